```python
import jax
import jax.numpy as jnp
from jax import lax
import numpy as np

D_MODEL = 1024
BATCH = 8
SEQ = 2048
DEPTH = 2
DEC_BATCH = 128
DEC_SEQ = 4
PAST_LEN = 2048
PAGE_SIZE = 128

N_HEADS = 16
HEAD_DIM = D_MODEL // N_HEADS
N_KV_HEADS = 4
GROUP = N_HEADS // N_KV_HEADS
KV_WIDTH = N_KV_HEADS * HEAD_DIM
L_CMP = 32
L_SLC = 64
N_SEL = 16
WINDOW = 512
N_BRANCH = 3
CMP_HIDDEN = 2 * HEAD_DIM
C_CONV = D_MODEL // 2
CONV_W = 31
D_FFN = -(-(8 * D_MODEL) // (3 * 256)) * 256
Q_WIDTH = N_HEADS * HEAD_DIM
N_IN = Q_WIDTH + 6 * KV_WIDTH + N_HEADS * N_BRANCH + 2 * C_CONV + 2 * D_MODEL
SPLITS = (Q_WIDTH,
          Q_WIDTH + 6 * KV_WIDTH,
          Q_WIDTH + 6 * KV_WIDTH + N_HEADS * N_BRANCH,
          Q_WIDTH + 6 * KV_WIDTH + N_HEADS * N_BRANCH + 2 * C_CONV)
ATTN_SCALE = HEAD_DIM ** -0.5
SLC_Q_BLOCK = 64
WIN_Q_BLOCK = 128
EPS = 1e-6
FORCE_BONUS = 1e4
NEG_SCORE = -1e9

kernel_name = 'nsa_conformer_hybrid_step'


def rms_norm(x, g):
    x32 = x.astype(jnp.float32)
    y = x32 * lax.rsqrt(jnp.mean(x32 * x32, axis=-1, keepdims=True) + EPS)
    return (y * g.astype(jnp.float32)).astype(x.dtype)


def layer_norm(x, g, b):
    x32 = x.astype(jnp.float32)
    xc = x32 - jnp.mean(x32, axis=-1, keepdims=True)
    var = jnp.mean(xc * xc, axis=-1, keepdims=True)
    y = xc * lax.rsqrt(var + EPS) * g.astype(jnp.float32) + b.astype(jnp.float32)
    return y.astype(x.dtype)


def masked_softmax(s, mask):
    s = jnp.where(mask, s.astype(jnp.float32), -jnp.inf)
    m = jnp.max(s, axis=-1, keepdims=True)
    m = jnp.where(jnp.isfinite(m), m, 0.0)
    p = jnp.exp(s - m)
    d = jnp.sum(p, axis=-1, keepdims=True)
    return p / jnp.where(d > 0, d, 1.0)


def dense_attend(q5, k, v, mask):
    s = jnp.einsum('btgjd,bsgd->btgjs', q5, k) * ATTN_SCALE
    p = masked_softmax(s, mask)
    return jnp.einsum('btgjs,bsgd->btgjd', p.astype(v.dtype), v), p


def window_mask(qpos, kpos):
    d = qpos[:, None] - kpos[None, :]
    return ((d >= 0) & (d < WINDOW) & (kpos[None, :] >= 0))[None, :, None, None, :]


def compress_blocks(rows, w1, b1, w2):
    B, ncb = rows.shape[:2]
    flat = rows.transpose(0, 1, 3, 2, 4).reshape(B, ncb, N_KV_HEADS, L_CMP * HEAD_DIM)
    h = jax.nn.gelu(jnp.einsum('bigf,fh->bigh', flat, w1) + b1)
    return jnp.einsum('bigh,hd->bigd', h, w2)


def _selected_chunk(q5, idx, qpos, k_blk, v_blk):
    B, Tc = q5.shape[:2]
    n_sel = idx.shape[-1]
    b_ix = jnp.arange(B)[:, None, None, None]
    g_ix = jnp.arange(N_KV_HEADS)[None, None, :, None]
    kg = k_blk[b_ix, g_ix, idx].reshape(B, Tc, N_KV_HEADS, n_sel * L_SLC, HEAD_DIM)
    vg = v_blk[b_ix, g_ix, idx].reshape(B, Tc, N_KV_HEADS, n_sel * L_SLC, HEAD_DIM)
    kpos = (idx[..., None] * L_SLC + jnp.arange(L_SLC)).reshape(B, Tc, N_KV_HEADS, n_sel * L_SLC)
    mask = (kpos <= qpos[None, :, None, None])[:, :, :, None, :]
    s = jnp.einsum('btgjd,btgkd->btgjk', q5, kg) * ATTN_SCALE
    p = masked_softmax(s, mask)
    return jnp.einsum('btgjk,btgkd->btgjd', p.astype(vg.dtype), vg)


def selected_attention(q5, idx, qpos, k_blk, v_blk):
    T = q5.shape[1]
    if T <= SLC_Q_BLOCK or T % SLC_Q_BLOCK:
        return _selected_chunk(q5, idx, qpos, k_blk, v_blk)
    nq = T // SLC_Q_BLOCK

    def to_chunks(a):
        return a.reshape(a.shape[0], nq, SLC_Q_BLOCK, *a.shape[2:]).swapaxes(0, 1)

    xs = (to_chunks(q5), to_chunks(idx), qpos.reshape(nq, SLC_Q_BLOCK))
    out = lax.map(lambda t: _selected_chunk(t[0], t[1], t[2], k_blk, v_blk), xs)
    return out.swapaxes(0, 1).reshape(q5.shape)


def nsa_global(q5, kv4, qpos, w_cmp1, b_cmp1, w_cmp2):
    B, S = kv4.shape[:2]
    T = q5.shape[1]
    s_pad = -(-S // L_SLC) * L_SLC
    kv4 = jnp.pad(kv4, ((0, 0), (0, s_pad - S), (0, 0), (0, 0), (0, 0)))
    n_cb, n_sb = s_pad // L_CMP, s_pad // L_SLC
    blocks = kv4.reshape(B, n_cb, L_CMP, 4, N_KV_HEADS, HEAD_DIM)
    k_c = compress_blocks(blocks[:, :, :, 0], w_cmp1[0], b_cmp1[0], w_cmp2[0])
    v_c = compress_blocks(blocks[:, :, :, 1], w_cmp1[1], b_cmp1[1], w_cmp2[1])
    cb_end = (jnp.arange(n_cb) + 1) * L_CMP - 1
    mask_c = (cb_end[None, :] <= qpos[:, None])[None, :, None, None, :]
    o_cmp, p_cmp = dense_attend(q5, k_c, v_c, mask_c)
    p_slc = p_cmp.reshape(B, T, N_KV_HEADS, GROUP, n_sb, L_SLC // L_CMP).sum(axis=(3, 5))
    sb = jnp.arange(n_sb)[None, :]
    cur = (qpos // L_SLC)[:, None]
    forced = (sb == 0) | (sb == cur) | (sb == cur - 1)
    valid = sb * L_SLC <= qpos[:, None]
    score = p_slc + jnp.where(forced, FORCE_BONUS, 0.0)[None, :, None, :]
    score = jnp.where(valid[None, :, None, :], score, NEG_SCORE)
    _, idx = lax.top_k(score, min(N_SEL, n_sb))
    k_blk = kv4[:, :, 2].reshape(B, n_sb, L_SLC, N_KV_HEADS, HEAD_DIM).transpose(0, 3, 1, 2, 4)
    v_blk = kv4[:, :, 3].reshape(B, n_sb, L_SLC, N_KV_HEADS, HEAD_DIM).transpose(0, 3, 1, 2, 4)
    o_slc = selected_attention(q5, idx, qpos, k_blk, v_blk)
    return o_cmp, o_slc


def window_prompt(q5, kw, vw):
    B, T = q5.shape[:2]
    n_prev = -(-WINDOW // WIN_Q_BLOCK)
    pad = n_prev * WIN_Q_BLOCK
    band = pad + WIN_Q_BLOCK
    kp = jnp.pad(kw, ((0, 0), (pad, 0), (0, 0), (0, 0)))
    vp = jnp.pad(vw, ((0, 0), (pad, 0), (0, 0), (0, 0)))
    nb = T // WIN_Q_BLOCK
    qb = q5.reshape(B, nb, WIN_Q_BLOCK, N_KV_HEADS, GROUP, HEAD_DIM).swapaxes(0, 1)

    def one(args):
        q_i, i = args
        start = i * WIN_Q_BLOCK
        k_i = lax.dynamic_slice_in_dim(kp, start, band, axis=1)
        v_i = lax.dynamic_slice_in_dim(vp, start, band, axis=1)
        qpos = start + jnp.arange(WIN_Q_BLOCK)
        kpos = start - pad + jnp.arange(band)
        o, _ = dense_attend(q_i, k_i, v_i, window_mask(qpos, kpos))
        return o

    out = lax.map(one, (qb, jnp.arange(nb)))
    return out.swapaxes(0, 1).reshape(q5.shape)


def split_projection(u, w_in):
    B, T = u.shape[:2]
    z = jnp.einsum('btd,dn->btn', u, w_in)
    q, kv, g, glu, mg = jnp.split(z, SPLITS, axis=-1)
    q5 = q.reshape(B, T, N_KV_HEADS, GROUP, HEAD_DIM)
    kv6 = kv.reshape(B, T, 6, N_KV_HEADS, HEAD_DIM)
    g3 = jax.nn.sigmoid(g.reshape(B, T, N_KV_HEADS, GROUP, N_BRANCH))
    a, b = jnp.split(glu, 2, axis=-1)
    conv_in = a * jax.nn.sigmoid(b)
    ga, gb = jnp.split(jax.nn.sigmoid(mg), 2, axis=-1)
    return q5, kv6, g3, conv_in, ga, gb


def nsa_combine(g3, o_cmp, o_slc, o_win):
    o = g3[..., 0:1] * o_cmp + g3[..., 1:2] * o_slc + g3[..., 2:3] * o_win
    return o.reshape(o.shape[0], o.shape[1], Q_WIDTH)


def conformer_conv(hist, lp):
    y = lax.conv_general_dilated(hist, lp['w_dw'][:, None, :], window_strides=(1,), padding='VALID',
                                 dimension_numbers=('NWC', 'WIO', 'NWC'),
                                 feature_group_count=C_CONV) + lp['b_dw']
    y = jax.nn.silu(layer_norm(y, lp['ln_conv_g'], lp['ln_conv_b']))
    return jnp.einsum('btc,cd->btd', y, lp['w_pw2'])


def merge_out(o_nsa, o_conv, ga, gb, w_out):
    return jnp.einsum('btd,de->bte', ga * o_nsa + gb * o_conv, w_out)


def mixer_prompt(u, lp):
    q5, kv6, g3, conv_in, ga, gb = split_projection(u, lp['w_in'])
    T = u.shape[1]
    qpos = jnp.arange(T)
    kv4 = kv6[:, :, :4]
    o_cmp, o_slc = nsa_global(q5, kv4, qpos, lp['w_cmp1'], lp['b_cmp1'], lp['w_cmp2'])
    o_win = window_prompt(q5, kv6[:, :, 4], kv6[:, :, 5])
    o_nsa = nsa_combine(g3, o_cmp, o_slc, o_win)
    hist = jnp.pad(conv_in, ((0, 0), (CONV_W - 1, 0), (0, 0)))
    o_conv = conformer_conv(hist, lp)
    y = merge_out(o_nsa, o_conv, ga, gb, lp['w_out'])
    wb = min(WINDOW, T)
    return y, (kv4, kv6[:, T - wb:, 4:], hist[:, hist.shape[1] - (CONV_W - 1):])


def mixer_sample(u, lp, cache_l, page_table, win_l, conv_l):
    q5, kv6, g3, conv_in, ga, gb = split_projection(u, lp['w_in'])
    B, T = u.shape[:2]
    past_len = page_table.shape[1] * cache_l.shape[1]
    qpos = past_len + jnp.arange(T)
    past = cache_l[page_table].reshape(B, past_len, 4, N_KV_HEADS, HEAD_DIM)
    kv4_full = jnp.concatenate([past, kv6[:, :, :4]], axis=1)
    o_cmp, o_slc = nsa_global(q5, kv4_full, qpos, lp['w_cmp1'], lp['b_cmp1'], lp['w_cmp2'])
    wb = win_l.shape[1]
    win_full = jnp.concatenate([win_l, kv6[:, :, 4:]], axis=1)
    kpos = past_len - wb + jnp.arange(wb + T)
    o_win, _ = dense_attend(q5, win_full[:, :, 0], win_full[:, :, 1], window_mask(qpos, kpos))
    o_nsa = nsa_combine(g3, o_cmp, o_slc, o_win)
    hist = jnp.concatenate([conv_l, conv_in], axis=1)
    o_conv = conformer_conv(hist, lp)
    y = merge_out(o_nsa, o_conv, ga, gb, lp['w_out'])
    return y, (kv6[:, :, :4], win_full[:, T:], hist[:, T:])


def trunk_layer(x, c, mixer, lp):
    ada = jnp.einsum('bd,dn->bn', jax.nn.silu(c), lp['w_ada']) + lp['b_ada']
    sh_m, sc_m, gt_m, sh_f, sc_f, gt_f = [a[:, None, :] for a in jnp.split(ada, 6, axis=-1)]
    g = lp['norm_gain']
    h = rms_norm(x, g[0]) * (1.0 + sc_m) + sh_m
    y, st = mixer(h)
    x = x + gt_m * rms_norm(y, g[1])
    h = rms_norm(x, g[2]) * (1.0 + sc_f) + sh_f
    a, b = jnp.split(jnp.einsum('btd,df->btf', h, lp['w_ffn_in']), 2, axis=-1)
    y = jnp.einsum('btf,fd->btd', jax.nn.silu(a) * b, lp['w_ffn_out'])
    x = x + gt_f * rms_norm(y, g[3])
    return x, st


def setup_inputs(seed: int = 0) -> dict:
    key = jax.random.key(seed)
    ks = jax.random.split(key, 24)

    def nrm(k, shape, scale=1.0):
        return jax.random.normal(k, shape, jnp.float32) * scale

    n_pages = PAST_LEN // PAGE_SIZE
    n_used = DEC_BATCH * n_pages
    n_pool = n_used + (n_used + 3) // 4
    wb = min(WINDOW, PAST_LEN)
    page_table = jax.random.permutation(ks[5], n_pool)[:n_used].reshape(DEC_BATCH, n_pages).astype(jnp.int32)
    return {
        'x_prompt': nrm(ks[0], (BATCH, SEQ, D_MODEL)),
        'x_sample': nrm(ks[1], (DEC_BATCH, DEC_SEQ, D_MODEL)),
        'cache_kv': nrm(ks[2], (DEPTH, n_pool, PAGE_SIZE, 4, N_KV_HEADS, HEAD_DIM)),
        'state_win': nrm(ks[3], (DEPTH, DEC_BATCH, wb, 2, N_KV_HEADS, HEAD_DIM)),
        'state_conv': nrm(ks[4], (DEPTH, DEC_BATCH, CONV_W - 1, C_CONV), 0.5),
        'page_table': page_table,
        'c_prompt': nrm(ks[6], (BATCH, D_MODEL)),
        'c_sample': nrm(ks[7], (DEC_BATCH, D_MODEL)),
        'w_ada': nrm(ks[8], (DEPTH, D_MODEL, 6 * D_MODEL), 0.5 * D_MODEL ** -0.5),
        'b_ada': nrm(ks[9], (DEPTH, 6 * D_MODEL), 0.01),
        'norm_gain': 1.0 + nrm(ks[10], (DEPTH, 4, D_MODEL), 0.02),
        'w_in': nrm(ks[11], (DEPTH, D_MODEL, N_IN), D_MODEL ** -0.5),
        'w_cmp1': nrm(ks[12], (DEPTH, 2, L_CMP * HEAD_DIM, CMP_HIDDEN), (L_CMP * HEAD_DIM) ** -0.5),
        'b_cmp1': nrm(ks[13], (DEPTH, 2, CMP_HIDDEN), 0.01),
        'w_cmp2': nrm(ks[14], (DEPTH, 2, CMP_HIDDEN, HEAD_DIM), CMP_HIDDEN ** -0.5),
        'w_dw': nrm(ks[15], (DEPTH, CONV_W, C_CONV), CONV_W ** -0.5),
        'b_dw': nrm(ks[16], (DEPTH, C_CONV), 0.01),
        'ln_conv_g': 1.0 + nrm(ks[17], (DEPTH, C_CONV), 0.02),
        'ln_conv_b': nrm(ks[18], (DEPTH, C_CONV), 0.01),
        'w_pw2': nrm(ks[19], (DEPTH, C_CONV, D_MODEL), C_CONV ** -0.5),
        'w_out': nrm(ks[20], (DEPTH, D_MODEL, D_MODEL), D_MODEL ** -0.5),
        'w_ffn_in': nrm(ks[21], (DEPTH, D_MODEL, 2 * D_FFN), D_MODEL ** -0.5),
        'w_ffn_out': nrm(ks[22], (DEPTH, D_FFN, D_MODEL), D_FFN ** -0.5),
    }


def reference(x_prompt, x_sample, cache_kv, state_win, state_conv, page_table, c_prompt, c_sample,
              w_ada, b_ada, norm_gain, w_in, w_cmp1, b_cmp1, w_cmp2, w_dw, b_dw, ln_conv_g,
              ln_conv_b, w_pw2, w_out, w_ffn_in, w_ffn_out):
    y_p, y_s = x_prompt, x_sample
    kv_p, kv_s, win_p, win_s, conv_p, conv_s = [], [], [], [], [], []
    for l in range(DEPTH):
        lp = {'w_ada': w_ada[l], 'b_ada': b_ada[l], 'norm_gain': norm_gain[l], 'w_in': w_in[l],
              'w_cmp1': w_cmp1[l], 'b_cmp1': b_cmp1[l], 'w_cmp2': w_cmp2[l], 'w_dw': w_dw[l],
              'b_dw': b_dw[l], 'ln_conv_g': ln_conv_g[l], 'ln_conv_b': ln_conv_b[l],
              'w_pw2': w_pw2[l], 'w_out': w_out[l], 'w_ffn_in': w_ffn_in[l], 'w_ffn_out': w_ffn_out[l]}
        y_p, st_p = trunk_layer(y_p, c_prompt, lambda u: mixer_prompt(u, lp), lp)
        y_s, st_s = trunk_layer(
            y_s, c_sample,
            lambda u: mixer_sample(u, lp, cache_kv[l], page_table, state_win[l], state_conv[l]), lp)
        kv_p.append(st_p[0]); win_p.append(st_p[1]); conv_p.append(st_p[2])
        kv_s.append(st_s[0]); win_s.append(st_s[1]); conv_s.append(st_s[2])
    kv_prompt = jnp.stack(kv_p)
    kv_sample = jnp.stack(kv_s)
    win_prompt = jnp.stack(win_p)
    win_sample = jnp.stack(win_s)
    conv_prompt = jnp.stack(conv_p)
    conv_sample = jnp.stack(conv_s)
    return (y_p, y_s, kv_prompt, kv_sample, win_prompt, win_sample, conv_prompt, conv_sample)
```

```python
import functools

import jax
import jax.numpy as jnp
import numpy as np
from jax import lax
from jax.experimental import pallas as pl
from jax.experimental.pallas import tpu as pltpu

D_MODEL = 1024
DEPTH = 2
N_HEADS = 16
HEAD_DIM = 64
N_KV = 4
GROUP = 4
KV_WIDTH = N_KV * HEAD_DIM
L_CMP = 32
L_SLC = 64
N_SEL = 16
WINDOW = 512
N_BRANCH = 3
CMP_HIDDEN = 128
C_CONV = 512
CONV_W = 31
D_FFN = 2816
EPS = 1e-6
FORCE_BONUS = 1e4
NEG_SCORE = -1e9
ATTN_SCALE = HEAD_DIM ** -0.5

MASK_BIAS = -(2.0 ** 100)
GELU_C = float(np.float32(np.sqrt(2.0 / np.pi)))

LANE = 128
VMEM_LIMIT = 56 * 1024 * 1024

F32 = jnp.float32
BF16 = jnp.bfloat16

_OFF_Q = 0
_OFF_KV = 1024
_OFF_GLU = 2560
_OFF_MG = 3584
_OFF_G = 5632
_N_PROJ = 5760


def _cparams(sem):
    return pltpu.CompilerParams(dimension_semantics=sem, vmem_limit_bytes=VMEM_LIMIT)


def _dot(a, b):
    return jnp.dot(a, b, preferred_element_type=F32)


def _dot_nt(a, b):
    return lax.dot_general(a, b, (((1,), (1,)), ((), ())), preferred_element_type=F32)


def _sigmoid(x):
    return 1.0 / (1.0 + jnp.exp(-x))


def _rms(x, g):
    return x * lax.rsqrt(jnp.mean(x * x, axis=-1, keepdims=True) + EPS) * g


def _const_spec(shape):
    nd = len(shape)
    return pl.BlockSpec(shape, lambda *_: (0,) * nd)


def _ada_kernel(c_ref, w_ref, b_ref, o_ref):
    c = c_ref[...]
    s = (c * _sigmoid(c)).astype(BF16)
    o_ref[...] = _dot(s, w_ref[...].astype(BF16)) + b_ref[...]


def _ada(c_all, w_ada, b_ada):
    n = c_all.shape[0]
    tn = 1536
    return pl.pallas_call(
        _ada_kernel,
        grid=(DEPTH, 6 * D_MODEL // tn),
        in_specs=[
            pl.BlockSpec((n, D_MODEL), lambda l, j: (0, 0)),
            pl.BlockSpec((None, D_MODEL, tn), lambda l, j: (l, 0, j)),
            pl.BlockSpec((None, 1, tn), lambda l, j: (l, 0, j)),
        ],
        out_specs=pl.BlockSpec((None, n, tn), lambda l, j: (l, 0, j)),
        out_shape=jax.ShapeDtypeStruct((DEPTH, n, 6 * D_MODEL), F32),
        compiler_params=_cparams(("parallel", "parallel")),
        name="ada",
    )(c_all, w_ada, b_ada.reshape(DEPTH, 1, 6 * D_MODEL))


def _proj_kernel(x_ref, sc_ref, sh_ref, g_ref, w_ref,
                 q_ref, kv_ref, win_ref, conv_ref, ga_ref, gb_ref, gate_ref):
    x = x_ref[...]
    h = (_rms(x, g_ref[...]) * (1.0 + sc_ref[...]) + sh_ref[...]).astype(BF16)
    q_ref[...] = _dot(h, w_ref[:, _OFF_Q:_OFF_KV]).astype(BF16)
    kv = _dot(h, w_ref[:, _OFF_KV:_OFF_GLU])
    kv_ref[...] = kv[:, :4 * KV_WIDTH]
    win_ref[...] = kv[:, 4 * KV_WIDTH:]
    glu = _dot(h, w_ref[:, _OFF_GLU:_OFF_MG])
    conv_ref[...] = glu[:, :C_CONV] * _sigmoid(glu[:, C_CONV:])
    mg = _sigmoid(_dot(h, w_ref[:, _OFF_MG:_OFF_G]))
    ga_ref[...] = mg[:, :D_MODEL]
    gb_ref[...] = mg[:, D_MODEL:]
    gate_ref[...] = _sigmoid(_dot(h, w_ref[:, _OFF_G:_N_PROJ]))


def _mod_spec(mod, tm, rows_per_batch):
    if mod.ndim == 3:
        return pl.BlockSpec((None, 1, D_MODEL), lambda i: (i * tm // rows_per_batch, 0, 0))
    return pl.BlockSpec((tm, D_MODEL), lambda i: (i, 0))


def _proj(x, sc, sh, gain, w, tm, rows_per_batch):
    n = x.shape[0]
    row = lambda c: pl.BlockSpec((tm, c), lambda i: (i, 0))
    return pl.pallas_call(
        _proj_kernel,
        grid=(n // tm,),
        in_specs=[row(D_MODEL), _mod_spec(sc, tm, rows_per_batch), _mod_spec(sh, tm, rows_per_batch),
                  _const_spec((1, D_MODEL)), _const_spec((D_MODEL, _N_PROJ))],
        out_specs=[row(D_MODEL), row(4 * KV_WIDTH), row(2 * KV_WIDTH), row(C_CONV),
                   row(D_MODEL), row(D_MODEL), row(LANE)],
        out_shape=[jax.ShapeDtypeStruct((n, D_MODEL), BF16),
                   jax.ShapeDtypeStruct((n, 4 * KV_WIDTH), F32),
                   jax.ShapeDtypeStruct((n, 2 * KV_WIDTH), F32),
                   jax.ShapeDtypeStruct((n, C_CONV), F32),
                   jax.ShapeDtypeStruct((n, D_MODEL), F32),
                   jax.ShapeDtypeStruct((n, D_MODEL), F32),
                   jax.ShapeDtypeStruct((n, LANE), F32)],
        compiler_params=_cparams(("parallel",)),
        name="proj",
    )(x, sc, sh, gain, w)


def _proj_weight(w_in_l):
    q = w_in_l[:, :1024] * ATTN_SCALE
    kv = w_in_l[:, 1024:2560]
    g = w_in_l[:, 2560:2608]
    glu = w_in_l[:, 2608:3632]
    mg = w_in_l[:, 3632:5680]
    pad = jnp.zeros((D_MODEL, _N_PROJ - _OFF_G - 48), F32)
    return jnp.concatenate([q, kv, glu, mg, g, pad], axis=1).astype(BF16)


_POS_PAIRS = L_CMP // 2


def _gelu_tanh(x):
    return x * (0.5 * (1.0 + jnp.tanh(GELU_C * (x + 0.044715 * (x * x * x)))))


def _compress_kernel(xe_ref, xo_ref, wc_ref, b1_ref, w2_ref, kc_ref, vc_ref, acc_ref):
    pp = pl.program_id(1)

    @pl.when(pp == 0)
    def _():
        acc_ref[...] = jnp.zeros_like(acc_ref)

    for kv in range(2):
        for gh in range(2):
            c0 = kv * KV_WIDTH + gh * LANE
            chunk = jnp.concatenate([xe_ref[:, c0:c0 + LANE], xo_ref[:, c0:c0 + LANE]], axis=1)
            a0 = (kv * 2 + gh) * 256
            acc_ref[:, a0:a0 + 256] += _dot(chunk.astype(BF16), wc_ref[kv])

    @pl.when(pp == _POS_PAIRS - 1)
    def _():
        for kv, o_ref in ((0, kc_ref), (1, vc_ref)):
            halves = []
            for gh in range(2):
                a0 = (kv * 2 + gh) * 256
                h = _gelu_tanh(acc_ref[:, a0:a0 + 256] + b1_ref[kv])
                halves.append(_dot(h.astype(BF16), w2_ref[kv]))
            o_ref[...] = jnp.concatenate(halves, axis=1)


def _compress_weights(w1, b1, w2):
    eye2 = jnp.eye(2, dtype=F32)
    w1r = w1.reshape(2, _POS_PAIRS, 2, HEAD_DIM, CMP_HIDDEN)
    wc = jnp.einsum("kpidh,gG->kpigdGh", w1r, eye2).reshape(2, _POS_PAIRS, 256, 256).astype(BF16)
    b1t = jnp.concatenate([b1, b1], axis=-1).reshape(2, 1, 256)
    w2e = jnp.einsum("khd,gG->kghGd", w2, eye2).reshape(2, 256, LANE).astype(BF16)
    return wc, b1t, w2e


def _compress(x2, wc, b1t, w2e, r):
    nb = x2.shape[0]
    blk = lambda off: pl.BlockSpec((r, 2 * KV_WIDTH), lambda i, pp: (i, 4 * pp + off))
    out = pl.BlockSpec((r, KV_WIDTH), lambda i, pp: (i, 0))
    return pl.pallas_call(
        _compress_kernel,
        grid=(nb // r, _POS_PAIRS),
        in_specs=[blk(0), blk(2),
                  pl.BlockSpec((2, None, 256, 256), lambda i, pp: (0, pp, 0, 0)),
                  _const_spec((2, 1, 256)), _const_spec((2, 256, LANE))],
        out_specs=[out, out],
        out_shape=[jax.ShapeDtypeStruct((nb, KV_WIDTH), F32)] * 2,
        scratch_shapes=[pltpu.VMEM((r, 1024), F32)],
        compiler_params=_cparams(("parallel", "arbitrary")),
        name="compress",
    )(x2, x2, wc, b1t, w2e)


def _topk_member(score, n_rows):
    sb = lax.broadcasted_iota(jnp.int32, score.shape, 0)
    cnt = jnp.zeros(score.shape, jnp.int32)
    for jj in range(n_rows):
        row = score[jj:jj + 1, :]
        beats = (row > score) | ((row == score) & (jj < sb))
        cnt = cnt + jnp.where(beats, 1, 0)
    return cnt < N_SEL


_TQ = 256


def _select_kernel(q_ref, kcp_ref, qa_ref):
    t0 = pl.program_id(2) * _TQ
    qpos = t0 + lax.broadcasted_iota(jnp.int32, (1, _TQ), 1)
    r = lax.broadcasted_iota(jnp.int32, (2 * 32, 1), 0)
    blk = jnp.where(r < 32, 2 * r, 2 * (r - 32) + 1)
    mask_c = ((blk + 1) * L_CMP - 1) <= qpos
    q = q_ref[...]
    kcp = kcp_ref[...]
    p_slc = jnp.zeros((32, _TQ), F32)
    for j in range(GROUP):
        s = _dot_nt(kcp, q[:, j * HEAD_DIM:(j + 1) * HEAD_DIM])
        s = jnp.where(mask_c, s, -jnp.inf)
        m = jnp.max(s, axis=0, keepdims=True)
        m = jnp.where(m == -jnp.inf, 0.0, m)
        p = jnp.exp(s - m)
        d = jnp.sum(p, axis=0, keepdims=True)
        p = p / jnp.where(d > 0, d, 1.0)
        p_slc = p_slc + (p[:32] + p[32:])
    sb = lax.broadcasted_iota(jnp.int32, (32, 1), 0)
    cur = qpos // L_SLC
    forced = (sb == 0) | (sb == cur) | (sb == cur - 1)
    score = p_slc + jnp.where(forced, FORCE_BONUS, 0.0)
    score = jnp.where(sb * L_SLC <= qpos, score, NEG_SCORE)
    sel = _topk_member(score, 32)
    bias_t = jnp.where(sel, 0.0, MASK_BIAS)
    bias = jnp.concatenate([bias_t, jnp.zeros((LANE - 32, _TQ), F32)], axis=0).T
    for j in range(GROUP):
        qh = q[:, j * HEAD_DIM:(j + 1) * HEAD_DIM].astype(F32)
        qa_ref[j] = jnp.concatenate([qh, bias[:, :LANE - HEAD_DIM]], axis=1).astype(BF16)


def _select(q, kcp, b, t):
    return pl.pallas_call(
        _select_kernel,
        grid=(b, N_KV, t // _TQ),
        in_specs=[pl.BlockSpec((None, _TQ, KV_WIDTH), lambda bi, g, i: (bi, i, g)),
                  pl.BlockSpec((None, None, 64, HEAD_DIM), lambda bi, g, i: (bi, g, 0, 0))],
        out_specs=pl.BlockSpec((None, None, GROUP, _TQ, LANE), lambda bi, g, i: (bi, g, 0, i, 0)),
        out_shape=jax.ShapeDtypeStruct((b, N_KV, GROUP, t, LANE), BF16),
        compiler_params=_cparams(("parallel", "parallel", "parallel")),
        name="select",
    )(q, kcp)


_TK = 256
_ROWS = GROUP * _TQ


def _online_step(s, v, m_ref, l_ref, acc_ref):
    m_old = m_ref[...]
    m_new = jnp.maximum(m_old, jnp.max(s, axis=-1, keepdims=True))
    alpha = jnp.exp(m_old - m_new)
    p = jnp.exp(s - m_new)
    l_ref[...] = alpha * l_ref[...] + jnp.sum(p, axis=-1, keepdims=True)
    acc_ref[...] = alpha * acc_ref[...] + _dot(p.astype(BF16), v)
    m_ref[...] = m_new


def _attn_kernel(qa_ref, ks_ref, vs_ref, kw_ref, vw_ref, kc_ref, vc_ref, g_ref, o_ref,
                 m_ref, l_ref, acc_ref):
    qi = pl.program_id(2)
    t0 = qi * _TQ
    q = qa_ref[...].reshape(_ROWS, LANE)
    row_t = t0 + (lax.broadcasted_iota(jnp.int32, (_ROWS, 1), 0) & (_TQ - 1))

    s = _dot_nt(q, kc_ref[...])
    cb_end = (lax.broadcasted_iota(jnp.int32, (1, 64), 1) + 1) * L_CMP - 1
    s = jnp.where(cb_end <= row_t, s, -jnp.inf)
    m = jnp.max(s, axis=-1, keepdims=True)
    m = jnp.where(m == -jnp.inf, 0.0, m)
    p = jnp.exp(s - m)
    d = jnp.sum(p, axis=-1, keepdims=True)
    p = p / jnp.where(d > 0, d, 1.0)
    o_cmp = _dot(p.astype(BF16), vc_ref[...])

    def reset():
        m_ref[...] = jnp.full(m_ref.shape, -jnp.inf, F32)
        l_ref[...] = jnp.zeros(l_ref.shape, F32)
        acc_ref[...] = jnp.zeros(acc_ref.shape, F32)

    kiota = lax.broadcasted_iota(jnp.int32, (1, _TK), 1)

    reset()

    def slc_body(i, carry):
        k0 = pl.multiple_of((qi - i) * _TK, _TK)
        s = _dot_nt(q, ks_ref[pl.ds(k0, _TK), :])
        s = jnp.where(k0 + kiota <= row_t, s, -jnp.inf)
        _online_step(s, vs_ref[pl.ds(k0, _TK), :], m_ref, l_ref, acc_ref)
        return carry

    lax.fori_loop(0, qi + 1, slc_body, 0)
    o_slc = acc_ref[...] / l_ref[...]

    reset()

    def win_body(i, carry):
        k0 = pl.multiple_of((qi - i) * _TK, _TK)
        s = _dot_nt(q, kw_ref[pl.ds(k0, _TK), :])
        dist = row_t - (k0 + kiota)
        s = jnp.where((dist >= 0) & (dist < WINDOW), s, -jnp.inf)
        _online_step(s, vw_ref[pl.ds(k0, _TK), :], m_ref, l_ref, acc_ref)
        return carry

    lax.fori_loop(0, jnp.minimum(qi, WINDOW // _TK) + 1, win_body, 0)
    o_win = acc_ref[...] / l_ref[...]

    g = g_ref[...]
    heads = []
    for j in range(GROUP):
        rs = slice(j * _TQ, (j + 1) * _TQ)
        c = N_BRANCH * j
        heads.append(g[:, c:c + 1] * o_cmp[rs] + g[:, c + 1:c + 2] * o_slc[rs]
                     + g[:, c + 2:c + 3] * o_win[rs])
    o_ref[...] = jnp.concatenate(heads, axis=1)


def _attn(qa, ks, vs, kw, vw, kc, vc, gates, b, t):
    per_head = lambda rows, cols: pl.BlockSpec((None, None, rows, cols), lambda bi, g, i: (bi, g, 0, 0))
    return pl.pallas_call(
        _attn_kernel,
        grid=(b, N_KV, t // _TQ),
        in_specs=[pl.BlockSpec((None, None, GROUP, _TQ, LANE), lambda bi, g, i: (bi, g, 0, i, 0)),
                  per_head(t, LANE), per_head(t, HEAD_DIM), per_head(t, LANE), per_head(t, HEAD_DIM),
                  per_head(64, LANE), per_head(64, HEAD_DIM),
                  pl.BlockSpec((None, None, _TQ, LANE), lambda bi, g, i: (bi, g, i, 0))],
        out_specs=pl.BlockSpec((None, _TQ, KV_WIDTH), lambda bi, g, i: (bi, i, g)),
        out_shape=jax.ShapeDtypeStruct((b, t, D_MODEL), F32),
        scratch_shapes=[pltpu.VMEM((_ROWS, 1), F32), pltpu.VMEM((_ROWS, 1), F32),
                        pltpu.VMEM((_ROWS, HEAD_DIM), F32)],
        compiler_params=_cparams(("parallel", "parallel", "arbitrary")),
        name="attn",
    )(qa, ks, vs, kw, vw, kc, vc, gates)


_N_PAGES = 16
_PAGE = 128
_PAST = _N_PAGES * _PAGE
_SROWS = GROUP * N_KV * 4
_NEWPAD = 128
_NSB_PAD = 40
_NEWROWS = 8


def _diag_heads(o, row_g):
    col_g = lax.broadcasted_iota(jnp.int32, (1, KV_WIDTH), 1) // HEAD_DIM
    om = jnp.where(col_g == row_g, o, 0.0)
    return (om[:, 0:64] + om[:, 64:128]) + (om[:, 128:192] + om[:, 192:256])


def _softmax_rows(s):
    m = jnp.max(s, axis=-1, keepdims=True)
    m = jnp.where(m == -jnp.inf, 0.0, m)
    p = jnp.exp(s - m)
    d = jnp.sum(p, axis=-1, keepdims=True)
    return p / jnp.where(d > 0, d, 1.0)


def _sample_attn_kernel(dt, pt_ref, qbd_ref, *refs):
    del pt_ref
    page_refs = refs[:_N_PAGES]
    sum_refs = refs[_N_PAGES:2 * _N_PAGES]
    kvn_ref, wold_ref, wnew_ref, g_ref, oh_ref, o_ref, sum_scr, kcp_scr = refs[2 * _N_PAGES:]

    qbd = qbd_ref[...]
    rows = lax.broadcasted_iota(jnp.int32, (_SROWS, 1), 0)
    row_t = rows % dt
    row_g = (rows // dt) % N_KV

    per_page = _PAGE // L_CMP
    for j in range(_N_PAGES):
        sum_scr[per_page * j:per_page * (j + 1), :] = sum_refs[j][...]
        for i in range(per_page):
            dst = (i % 2) * 32 + (per_page * j + i) // 2
            kcp_scr[dst:dst + 1, :] = sum_refs[j][i:i + 1, :KV_WIDTH]
    kvc = sum_scr[...]
    p = _softmax_rows(_dot_nt(qbd, kvc[:, :KV_WIDTH].astype(BF16)))
    o_cmp = _diag_heads(_dot(p.astype(BF16), kvc[:, KV_WIDTH:].astype(BF16)), row_g)

    s_t = _dot_nt(kcp_scr[...].astype(BF16), qbd)
    m = jnp.max(s_t, axis=0, keepdims=True)
    pt = jnp.exp(s_t - m)
    pt = pt / jnp.sum(pt, axis=0, keepdims=True)
    pp = pt[:32] + pt[32:]
    nq = N_KV * dt
    p_slc = (pp[:, 0:nq] + pp[:, nq:2 * nq]) + (pp[:, 2 * nq:3 * nq] + pp[:, 3 * nq:4 * nq])
    p_slc = jnp.concatenate([p_slc, jnp.zeros((_NSB_PAD - 32, nq), F32)], axis=0)
    sb = lax.broadcasted_iota(jnp.int32, (_NSB_PAD, 1), 0)
    qpos = _PAST + lax.broadcasted_iota(jnp.int32, (1, nq), 1) % dt
    cur = qpos // L_SLC
    forced = (sb == 0) | (sb == cur) | (sb == cur - 1)
    score = p_slc + jnp.where(forced, FORCE_BONUS, 0.0)
    score = jnp.where(sb * L_SLC <= qpos, score, NEG_SCORE)
    score = jnp.where(sb < _PAST // L_SLC + 1, score, -jnp.inf)
    sel_t = _topk_member(score, _PAST // L_SLC + 1)
    bias_t = jnp.where(sel_t, 0.0, MASK_BIAS)
    bias_t = jnp.concatenate([bias_t, jnp.zeros((LANE - _NSB_PAD, nq), F32)], axis=0)
    bias_t = jnp.concatenate([bias_t, jnp.zeros((LANE, LANE - nq), F32)], axis=1)
    bias = bias_t.T[:nq]
    bias = jnp.concatenate([bias] * GROUP, axis=0).astype(BF16)

    knew = kvn_ref[...]
    zpad = jnp.zeros((_NEWPAD - _NEWROWS, KV_WIDTH), F32)
    k_all = jnp.concatenate([r[:, :KV_WIDTH] for r in page_refs]
                            + [knew[:, 2 * KV_WIDTH:3 * KV_WIDTH], zpad], axis=0).astype(BF16)
    v_all = jnp.concatenate([r[:, KV_WIDTH:] for r in page_refs]
                            + [knew[:, 3 * KV_WIDTH:], zpad], axis=0).astype(BF16)
    s = _dot_nt(qbd, k_all) + _dot(bias, oh_ref[...])
    kidx = lax.broadcasted_iota(jnp.int32, (1, _PAST + _NEWPAD), 1)
    s = jnp.where((kidx < _PAST) | (kidx - _PAST <= row_t), s, -jnp.inf)
    o_slc = _diag_heads(_dot(_softmax_rows(s).astype(BF16), v_all), row_g)

    wold = wold_ref[...]
    wnew = wnew_ref[...]
    wb = wold.shape[0]
    kw = jnp.concatenate([wold[:, :KV_WIDTH], wnew[:, :KV_WIDTH], zpad], axis=0).astype(BF16)
    vw = jnp.concatenate([wold[:, KV_WIDTH:], wnew[:, KV_WIDTH:], zpad], axis=0).astype(BF16)
    s = _dot_nt(qbd, kw)
    widx = lax.broadcasted_iota(jnp.int32, (1, wb + _NEWPAD), 1)
    dist = jnp.where(widx < wb, row_t + (wb - widx), row_t - (widx - wb))
    s = jnp.where((dist >= 0) & (dist < WINDOW), s, -jnp.inf)
    o_win = _diag_heads(_dot(_softmax_rows(s).astype(BF16), vw), row_g)

    g = g_ref[...]
    o_ref[...] = g[:, 0:1] * o_cmp + g[:, 1:2] * o_slc + g[:, 2:3] * o_win


def _sample_attn(page_table, qbd, cache3, kvc_pool, kvn, wold, wnew, grow, onehot):
    nb = qbd.shape[0]
    dt = kvn.shape[1]
    pad_rows = lambda a: jnp.pad(a, ((0, 0), (0, _NEWROWS - dt), (0, 0)))
    kvn, wnew = pad_rows(kvn), pad_rows(wnew)
    page = lambda j: pl.BlockSpec((None, _PAGE, 2 * KV_WIDTH),
                                  lambda b, pt: (pt[b * _N_PAGES + j], 0, 1))
    summ = lambda j: pl.BlockSpec((None, 4, 2 * KV_WIDTH),
                                  lambda b, pt: (pt[b * _N_PAGES + j], 0, 0))
    per_seq = lambda r, c: pl.BlockSpec((None, r, c), lambda b, pt: (b, 0, 0))
    grid_spec = pltpu.PrefetchScalarGridSpec(
        num_scalar_prefetch=1,
        grid=(nb,),
        in_specs=[per_seq(_SROWS, KV_WIDTH)]
        + [page(j) for j in range(_N_PAGES)] + [summ(j) for j in range(_N_PAGES)]
        + [per_seq(_NEWROWS, 4 * KV_WIDTH), per_seq(wold.shape[1], 2 * KV_WIDTH),
           per_seq(_NEWROWS, 2 * KV_WIDTH), per_seq(_SROWS, LANE),
           pl.BlockSpec((LANE, _PAST + _NEWPAD), lambda b, pt: (0, 0))],
        out_specs=per_seq(_SROWS, HEAD_DIM),
        scratch_shapes=[pltpu.VMEM((4 * _N_PAGES, 2 * KV_WIDTH), F32),
                        pltpu.VMEM((4 * _N_PAGES, KV_WIDTH), F32)],
    )
    return pl.pallas_call(
        functools.partial(_sample_attn_kernel, dt),
        grid_spec=grid_spec,
        out_shape=jax.ShapeDtypeStruct((nb, _SROWS, HEAD_DIM), F32),
        compiler_params=_cparams(("parallel",)),
        name="sample_attn",
    )(page_table.reshape(-1), qbd, *([cache3] * _N_PAGES), *([kvc_pool] * _N_PAGES),
      kvn, wold, wnew, grow, onehot)


_TT = 256
_HALO = 32
_RC = 64


def _dwconv_kernel(cur_ref, prev_ref, w_ref, b_ref, y_ref, h_ref):
    prev = prev_ref[...]
    h_ref[0:_HALO, :] = jnp.where(pl.program_id(1) == 0, 0.0, prev)
    h_ref[_HALO:, :] = cur_ref[...]
    off = _HALO - (CONV_W - 1)
    for r0 in range(0, _TT, _RC):
        acc = jnp.broadcast_to(b_ref[...], (_RC, C_CONV))
        for k in range(CONV_W):
            acc = acc + w_ref[k:k + 1, :] * h_ref[r0 + off + k:r0 + off + k + _RC, :]
        y_ref[r0:r0 + _RC, :] = acc


def _dwconv(conv_in, w_pad, b_dw):
    b, t, _ = conv_in.shape
    return pl.pallas_call(
        _dwconv_kernel,
        grid=(b, t // _TT),
        in_specs=[pl.BlockSpec((None, _TT, C_CONV), lambda bi, i: (bi, i, 0)),
                  pl.BlockSpec((None, _HALO, C_CONV),
                               lambda bi, i: (bi, jnp.maximum(i * (_TT // _HALO) - 1, 0), 0)),
                  _const_spec((_HALO, C_CONV)), _const_spec((1, C_CONV))],
        out_specs=pl.BlockSpec((None, _TT, C_CONV), lambda bi, i: (bi, i, 0)),
        out_shape=jax.ShapeDtypeStruct((b, t, C_CONV), F32),
        scratch_shapes=[pltpu.VMEM((_TT + _HALO, C_CONV), F32)],
        compiler_params=_cparams(("parallel", "parallel")),
        name="dwconv",
    )(conv_in, conv_in, w_pad, b_dw)


_SB = 8


def _dwconv_sample_kernel(h_ref, w_ref, b_ref, y_ref):
    dt = y_ref.shape[0]
    w = w_ref[0:CONV_W, :]
    for t in range(dt):
        y_ref[t:t + 1, :] = jnp.sum(h_ref[t:t + CONV_W, :] * w, axis=0, keepdims=True) + b_ref[...]


def _dwconv_sample(hist, w_pad, b_dw):
    b, rows, c = hist.shape
    dt = rows - (CONV_W - 1)
    h2 = hist.transpose(1, 0, 2).reshape(rows, b * c)
    w_t = jnp.tile(w_pad, (1, _SB))
    b_t = jnp.tile(b_dw, (1, _SB))
    y = pl.pallas_call(
        _dwconv_sample_kernel,
        grid=(b // _SB,),
        in_specs=[pl.BlockSpec((rows, _SB * c), lambda i: (0, i)),
                  _const_spec((_HALO, _SB * c)), _const_spec((1, _SB * c))],
        out_specs=pl.BlockSpec((dt, _SB * c), lambda i: (0, i)),
        out_shape=jax.ShapeDtypeStruct((dt, b * c), F32),
        compiler_params=_cparams(("parallel",)),
        name="dwconv_sample",
    )(h2, w_t, b_t)
    return y.reshape(dt, b, c).transpose(1, 0, 2)


def _mix_out_kernel(y_ref, lng_ref, lnb_ref, wpw_ref, on_ref, ga_ref, gb_ref, wo_ref,
                    x_ref, gt_ref, gain_ref, o_ref):
    y = y_ref[...]
    yc = y - jnp.mean(y, axis=-1, keepdims=True)
    var = jnp.mean(yc * yc, axis=-1, keepdims=True)
    yn = yc * lax.rsqrt(var + EPS) * lng_ref[...] + lnb_ref[...]
    o_conv = _dot((yn * _sigmoid(yn)).astype(BF16), wpw_ref[...])
    merged = ga_ref[...] * on_ref[...] + gb_ref[...] * o_conv
    yo = _dot(merged.astype(BF16), wo_ref[...])
    o_ref[...] = x_ref[...] + gt_ref[...] * _rms(yo, gain_ref[...])


def _mix_out(ydw, lng, lnb, wpw, o_nsa, ga, gb, wo, x, gt, gain, tm, rows_per_batch):
    n = x.shape[0]
    row = lambda c: pl.BlockSpec((tm, c), lambda i: (i, 0))
    return pl.pallas_call(
        _mix_out_kernel,
        grid=(n // tm,),
        in_specs=[row(C_CONV), _const_spec((1, C_CONV)), _const_spec((1, C_CONV)),
                  _const_spec((C_CONV, D_MODEL)), row(D_MODEL), row(D_MODEL), row(D_MODEL),
                  _const_spec((D_MODEL, D_MODEL)), row(D_MODEL), _mod_spec(gt, tm, rows_per_batch),
                  _const_spec((1, D_MODEL))],
        out_specs=row(D_MODEL),
        out_shape=jax.ShapeDtypeStruct((n, D_MODEL), F32),
        compiler_params=_cparams(("parallel",)),
        name="mix_out",
    )(ydw, lng, lnb, wpw, o_nsa, ga, gb, wo, x, gt, gain)


_FC = D_FFN // 2


def _ffn_kernel(x_ref, sc_ref, sh_ref, gt_ref, g2_ref, g3_ref, wi_ref, wo_ref, o_ref):
    x = x_ref[...]
    h = (_rms(x, g2_ref[...]) * (1.0 + sc_ref[...]) + sh_ref[...]).astype(BF16)
    y = jnp.zeros(x.shape, F32)
    for c in range(0, D_FFN, _FC):
        a = _dot(h, wi_ref[:, c:c + _FC])
        b = _dot(h, wi_ref[:, D_FFN + c:D_FFN + c + _FC])
        u = (a * _sigmoid(a) * b).astype(BF16)
        y = y + _dot(u, wo_ref[c:c + _FC, :])
    o_ref[...] = x + gt_ref[...] * _rms(y, g3_ref[...])


def _ffn(x, sc, sh, gt, g2, g3, wi, wo, tm, rows_per_batch):
    n = x.shape[0]
    row = pl.BlockSpec((tm, D_MODEL), lambda i: (i, 0))
    mod = lambda a: _mod_spec(a, tm, rows_per_batch)
    return pl.pallas_call(
        _ffn_kernel,
        grid=(n // tm,),
        in_specs=[row, mod(sc), mod(sh), mod(gt), _const_spec((1, D_MODEL)), _const_spec((1, D_MODEL)),
                  _const_spec((D_MODEL, 2 * D_FFN)), _const_spec((D_FFN, D_MODEL))],
        out_specs=row,
        out_shape=jax.ShapeDtypeStruct((n, D_MODEL), F32),
        compiler_params=_cparams(("parallel",)),
        name="ffn",
    )(x, sc, sh, gt, g2, g3, wi, wo)


def _split_heads(a, b, t):
    return a.reshape(b, t, N_KV, HEAD_DIM).transpose(0, 2, 1, 3)


def _prompt_mixer(q, kv4, win, gates, kc, vc, b, t):
    n_sb = t // L_SLC
    onehot = (jnp.arange(t)[:, None] // L_SLC == jnp.arange(n_sb)[None, :]).astype(BF16)
    onehot = jnp.broadcast_to(onehot, (b, N_KV, t, n_sb))
    z = lambda c: jnp.zeros((b, N_KV, t, c), BF16)
    ks = jnp.concatenate([_split_heads(kv4[:, 512:768], b, t).astype(BF16), onehot,
                          z(LANE - HEAD_DIM - n_sb)], axis=-1)
    vs = _split_heads(kv4[:, 768:1024], b, t).astype(BF16)
    kw = jnp.concatenate([_split_heads(win[:, :256], b, t).astype(BF16), z(LANE - HEAD_DIM)], axis=-1)
    vw = _split_heads(win[:, 256:], b, t).astype(BF16)
    n_cb = t // L_CMP
    kch = _split_heads(kc, b, n_cb).astype(BF16)
    vch = _split_heads(vc, b, n_cb).astype(BF16)
    kcp = jnp.concatenate([kch[:, :, 0::2], kch[:, :, 1::2]], axis=2)
    kca = jnp.concatenate([kch, jnp.zeros((b, N_KV, n_cb, LANE - HEAD_DIM), BF16)], axis=-1)
    g4 = gates[:, :N_HEADS * N_BRANCH].reshape(b, t, N_KV, GROUP * N_BRANCH).transpose(0, 2, 1, 3)
    g4 = jnp.concatenate([g4, jnp.zeros((b, N_KV, t, LANE - GROUP * N_BRANCH), F32)], axis=-1)
    qa = _select(q.reshape(b, t, D_MODEL), kcp, b, t)
    o = _attn(qa, ks, vs, kw, vw, kca, vch, g4, b, t)
    return o.reshape(b * t, D_MODEL)


def _sample_mixer(q, kv4, win, gates, kvc_pool, cache3, page_table, state_win_l, b, dt):
    qr = q.reshape(b, dt, N_KV, GROUP, HEAD_DIM).transpose(0, 3, 2, 1, 4)
    eye = jnp.eye(N_KV, dtype=BF16)
    qbd = (qr[:, :, :, :, None, :] * eye[None, None, :, None, :, None]).reshape(b, _SROWS, KV_WIDTH)
    gr = gates[:, :N_HEADS * N_BRANCH].reshape(b, dt, N_KV, GROUP, N_BRANCH).transpose(0, 3, 2, 1, 4)
    gr = gr.reshape(b, _SROWS, N_BRANCH)
    gr = jnp.concatenate([gr, jnp.zeros((b, _SROWS, LANE - N_BRANCH), F32)], axis=-1)
    nkeys = _PAST + _NEWPAD
    onehot = (jnp.arange(nkeys)[None, :] // L_SLC == jnp.arange(LANE)[:, None]).astype(BF16)
    wb = state_win_l.shape[1]
    o = _sample_attn(page_table, qbd, cache3, kvc_pool, kv4.reshape(b, dt, 4 * KV_WIDTH),
                     state_win_l.reshape(b, wb, 2 * KV_WIDTH), win.reshape(b, dt, 2 * KV_WIDTH),
                     gr, onehot)
    o = o.reshape(b, GROUP, N_KV, dt, HEAD_DIM).transpose(0, 3, 2, 1, 4)
    return o.reshape(b * dt, D_MODEL)


def kernel(x_prompt, x_sample, cache_kv, state_win, state_conv, page_table, c_prompt, c_sample,
           w_ada, b_ada, norm_gain, w_in, w_cmp1, b_cmp1, w_cmp2, w_dw, b_dw, ln_conv_g,
           ln_conv_b, w_pw2, w_out, w_ffn_in, w_ffn_out):
    bp, t, _ = x_prompt.shape
    bs, dt, _ = x_sample.shape
    n_pool = cache_kv.shape[1]
    wb = state_win.shape[2]
    tm_p, tm_s = 256, 256

    ada = _ada(jnp.concatenate([c_prompt, c_sample], axis=0), w_ada, b_ada)

    xp = x_prompt.reshape(bp * t, D_MODEL)
    xs = x_sample.reshape(bs * dt, D_MODEL)
    kv_p, kv_s, win_p, win_s, conv_p, conv_s = [], [], [], [], [], []
    for l in range(DEPTH):
        mods_p = [m.reshape(bp, 1, D_MODEL) for m in jnp.split(ada[l, :bp], 6, axis=-1)]
        mods_s = [jnp.repeat(m, dt, axis=0) for m in jnp.split(ada[l, bp:], 6, axis=-1)]
        gain = norm_gain[l].reshape(4, 1, D_MODEL)
        wproj = _proj_weight(w_in[l])
        wc, b1t, w2e = _compress_weights(w_cmp1[l], b_cmp1[l], w_cmp2[l])
        w_pad = jnp.concatenate([w_dw[l], jnp.zeros((_HALO - CONV_W, C_CONV), F32)], axis=0)
        bdw = b_dw[l].reshape(1, C_CONV)
        lng = ln_conv_g[l].reshape(1, C_CONV)
        lnb = ln_conv_b[l].reshape(1, C_CONV)
        wpw = w_pw2[l].astype(BF16)
        wo = w_out[l].astype(BF16)
        wfi = w_ffn_in[l].astype(BF16)
        wfo = w_ffn_out[l].astype(BF16)

        sh_m, sc_m, gt_m, sh_f, sc_f, gt_f = mods_p
        q, kv4, win, conv_in, ga, gb, gates = _proj(xp, sc_m, sh_m, gain[0], wproj, tm_p, t)
        kc, vc = _compress(kv4.reshape(bp * t // L_CMP, L_CMP * 4 * KV_WIDTH), wc, b1t, w2e,
                           bp * t // L_CMP)
        o_nsa = _prompt_mixer(q, kv4, win, gates, kc, vc, bp, t)
        conv_in3 = conv_in.reshape(bp, t, C_CONV)
        ydw = _dwconv(conv_in3, w_pad, bdw).reshape(bp * t, C_CONV)
        xp = _mix_out(ydw, lng, lnb, wpw, o_nsa, ga, gb, wo, xp, gt_m, gain[1], tm_p, t)
        xp = _ffn(xp, sc_f, sh_f, gt_f, gain[2], gain[3], wfi, wfo, tm_p, t)
        kv_p.append(kv4.reshape(bp, t, 4, N_KV, HEAD_DIM))
        win_p.append(win.reshape(bp, t, 2, N_KV, HEAD_DIM)[:, t - min(WINDOW, t):])
        conv_p.append(conv_in3[:, t - (CONV_W - 1):])

        sh_m, sc_m, gt_m, sh_f, sc_f, gt_f = mods_s
        q, kv4, win, conv_in, ga, gb, gates = _proj(xs, sc_m, sh_m, gain[0], wproj, tm_s, dt)
        cache3 = cache_kv[l].reshape(n_pool, _PAGE, 4 * KV_WIDTH)
        kc_pool, vc_pool = _compress(cache3.reshape(n_pool * _PAGE // L_CMP, L_CMP * 4 * KV_WIDTH),
                                     wc, b1t, w2e, 512)
        kvc_pool = jnp.concatenate([kc_pool, vc_pool], axis=-1).reshape(n_pool, _PAGE // L_CMP,
                                                                        2 * KV_WIDTH)
        o_nsa = _sample_mixer(q, kv4, win, gates, kvc_pool, cache3, page_table, state_win[l], bs, dt)
        hist = jnp.concatenate([state_conv[l], conv_in.reshape(bs, dt, C_CONV)], axis=1)
        ydw = _dwconv_sample(hist, w_pad, bdw).reshape(bs * dt, C_CONV)
        xs = _mix_out(ydw, lng, lnb, wpw, o_nsa, ga, gb, wo, xs, gt_m, gain[1], tm_s, dt)
        xs = _ffn(xs, sc_f, sh_f, gt_f, gain[2], gain[3], wfi, wfo, tm_s, dt)
        kv_s.append(kv4.reshape(bs, dt, 4, N_KV, HEAD_DIM))
        win_full = jnp.concatenate([state_win[l], win.reshape(bs, dt, 2, N_KV, HEAD_DIM)], axis=1)
        win_s.append(win_full[:, dt:])
        conv_s.append(hist[:, dt:])

    return (xp.reshape(bp, t, D_MODEL), xs.reshape(bs, dt, D_MODEL),
            jnp.stack(kv_p), jnp.stack(kv_s), jnp.stack(win_p), jnp.stack(win_s),
            jnp.stack(conv_p), jnp.stack(conv_s))
```

```python
import functools

import jax
import jax.numpy as jnp
import numpy as np
from jax import lax
from jax.experimental import pallas as pl
from jax.experimental.pallas import tpu as pltpu

D_MODEL = 1024
DEPTH = 2
N_HEADS = 16
HEAD_DIM = 64
N_KV = 4
GROUP = 4
KV_WIDTH = N_KV * HEAD_DIM
L_CMP = 32
L_SLC = 64
N_SEL = 16
WINDOW = 512
N_BRANCH = 3
CMP_HIDDEN = 128
C_CONV = 512
CONV_W = 31
D_FFN = 2816
EPS = 1e-6
FORCE_BONUS = 1e4
NEG_SCORE = -1e9
ATTN_SCALE = HEAD_DIM ** -0.5

MASK_BIAS = -(2.0 ** 100)
GELU_C = float(np.float32(np.sqrt(2.0 / np.pi)))

LANE = 128
VMEM_LIMIT = 56 * 1024 * 1024

F32 = jnp.float32
BF16 = jnp.bfloat16

_OFF_Q = 0
_OFF_KV = 1024
_OFF_GLU = 2560
_OFF_MG = 3584
_OFF_G = 5632
_N_PROJ = 5760
_KV6 = 6 * KV_WIDTH


def _cparams(sem):
    return pltpu.CompilerParams(dimension_semantics=sem, vmem_limit_bytes=VMEM_LIMIT)


def _dot(a, b):
    return jnp.dot(a, b, preferred_element_type=F32)


def _dot_nt(a, b):
    return lax.dot_general(a, b, (((1,), (1,)), ((), ())), preferred_element_type=F32)


def _sigmoid(x):
    return 1.0 / (1.0 + jnp.exp(-x))


def _rms(x, g):
    return x * lax.rsqrt(jnp.mean(x * x, axis=-1, keepdims=True) + EPS) * g


def _const_spec(shape):
    nd = len(shape)
    return pl.BlockSpec(shape, lambda *_: (0,) * nd)


def _ada_kernel(c_ref, w_ref, b_ref, o_ref):
    c = c_ref[...]
    s = (c * _sigmoid(c)).astype(BF16)
    o_ref[...] = _dot(s, w_ref[...].astype(BF16)) + b_ref[...]


def _ada(c_all, w_ada, b_ada):
    n = c_all.shape[0]
    tn = 1536
    return pl.pallas_call(
        _ada_kernel,
        grid=(DEPTH, 6 * D_MODEL // tn),
        in_specs=[
            pl.BlockSpec((n, D_MODEL), lambda l, j: (0, 0)),
            pl.BlockSpec((None, D_MODEL, tn), lambda l, j: (l, 0, j)),
            pl.BlockSpec((None, 1, tn), lambda l, j: (l, 0, j)),
        ],
        out_specs=pl.BlockSpec((None, n, tn), lambda l, j: (l, 0, j)),
        out_shape=jax.ShapeDtypeStruct((DEPTH, n, 6 * D_MODEL), F32),
        compiler_params=_cparams(("parallel", "parallel")),
        name="ada",
    )(c_all, w_ada, b_ada.reshape(DEPTH, 1, 6 * D_MODEL))


def _proj_tail(h, w_ref, conv_ref, ga_ref, gb_ref, gate_ref):
    glu = _dot(h, w_ref[:, _OFF_GLU:_OFF_MG])
    conv_ref[...] = glu[:, :C_CONV] * _sigmoid(glu[:, C_CONV:])
    mg = _sigmoid(_dot(h, w_ref[:, _OFF_MG:_OFF_G]))
    ga_ref[...] = mg[:, :D_MODEL]
    gb_ref[...] = mg[:, D_MODEL:]
    gate_ref[...] = _sigmoid(_dot(h, w_ref[:, _OFF_G:_N_PROJ]))


def _proj_prompt_kernel(x_ref, sc_ref, sh_ref, g_ref, w_ref, wt_ref,
                        q_ref, kvt_ref, cmp_ref, conv_ref, ga_ref, gb_ref, gate_ref):
    h = (_rms(x_ref[...], g_ref[...]) * (1.0 + sc_ref[...]) + sh_ref[...]).astype(BF16)
    q_ref[...] = _dot(h, w_ref[:, _OFF_Q:_OFF_KV]).astype(BF16)
    kvt_ref[...] = _dot_nt(wt_ref[...], h)
    cmp_ref[...] = _dot(h, w_ref[:, _OFF_KV:_OFF_KV + 2 * KV_WIDTH])
    _proj_tail(h, w_ref, conv_ref, ga_ref, gb_ref, gate_ref)


def _proj_sample_kernel(x_ref, sc_ref, sh_ref, g_ref, w_ref,
                        q_ref, kv_ref, conv_ref, ga_ref, gb_ref, gate_ref):
    h = (_rms(x_ref[...], g_ref[...]) * (1.0 + sc_ref[...]) + sh_ref[...]).astype(BF16)
    q_ref[...] = _dot(h, w_ref[:, _OFF_Q:_OFF_KV]).astype(BF16)
    kv_ref[...] = _dot(h, w_ref[:, _OFF_KV:_OFF_GLU])
    _proj_tail(h, w_ref, conv_ref, ga_ref, gb_ref, gate_ref)


def _mod_spec(mod, tm, rows_per_batch):
    if mod.ndim == 3:
        return pl.BlockSpec((None, 1, D_MODEL), lambda i: (i * tm // rows_per_batch, 0, 0))
    return pl.BlockSpec((tm, D_MODEL), lambda i: (i, 0))


def _row_spec(tm, c):
    return pl.BlockSpec((tm, c), lambda i: (i, 0))


def _proj_tail_out(n, tm):
    specs = [_row_spec(tm, C_CONV), _row_spec(tm, D_MODEL), _row_spec(tm, D_MODEL), _row_spec(tm, LANE)]
    shapes = [jax.ShapeDtypeStruct((n, C_CONV), F32), jax.ShapeDtypeStruct((n, D_MODEL), F32),
              jax.ShapeDtypeStruct((n, D_MODEL), F32), jax.ShapeDtypeStruct((n, LANE), F32)]
    return specs, shapes


def _proj_prompt(x, sc, sh, gain, w, wt, tm, b, t):
    n = x.shape[0]
    per_b = t // tm
    tail_specs, tail_shapes = _proj_tail_out(n, tm)
    return pl.pallas_call(
        _proj_prompt_kernel,
        grid=(n // tm,),
        in_specs=[_row_spec(tm, D_MODEL), _mod_spec(sc, tm, t), _mod_spec(sh, tm, t),
                  _const_spec((1, D_MODEL)), _const_spec((D_MODEL, _N_PROJ)), _const_spec((_KV6, D_MODEL))],
        out_specs=[_row_spec(tm, D_MODEL),
                   pl.BlockSpec((None, _KV6, tm), lambda i: (i // per_b, 0, i % per_b)),
                   _row_spec(tm, 2 * KV_WIDTH)] + tail_specs,
        out_shape=[jax.ShapeDtypeStruct((n, D_MODEL), BF16),
                   jax.ShapeDtypeStruct((b, _KV6, t), F32),
                   jax.ShapeDtypeStruct((n, 2 * KV_WIDTH), F32)] + tail_shapes,
        compiler_params=_cparams(("parallel",)),
        name="proj",
    )(x, sc, sh, gain, w, wt)


def _proj_sample(x, sc, sh, gain, w, tm, rows_per_batch):
    n = x.shape[0]
    tail_specs, tail_shapes = _proj_tail_out(n, tm)
    return pl.pallas_call(
        _proj_sample_kernel,
        grid=(n // tm,),
        in_specs=[_row_spec(tm, D_MODEL), _mod_spec(sc, tm, rows_per_batch), _mod_spec(sh, tm, rows_per_batch),
                  _const_spec((1, D_MODEL)), _const_spec((D_MODEL, _N_PROJ))],
        out_specs=[_row_spec(tm, D_MODEL), _row_spec(tm, _KV6)] + tail_specs,
        out_shape=[jax.ShapeDtypeStruct((n, D_MODEL), BF16),
                   jax.ShapeDtypeStruct((n, _KV6), F32)] + tail_shapes,
        compiler_params=_cparams(("parallel",)),
        name="proj_sample",
    )(x, sc, sh, gain, w)


def _proj_weight(w_in_l):
    q = w_in_l[:, :1024] * ATTN_SCALE
    kv = w_in_l[:, 1024:2560]
    g = w_in_l[:, 2560:2608]
    glu = w_in_l[:, 2608:3632]
    mg = w_in_l[:, 3632:5680]
    pad = jnp.zeros((D_MODEL, _N_PROJ - _OFF_G - 48), F32)
    return jnp.concatenate([q, kv, glu, mg, g, pad], axis=1).astype(BF16), kv.T.astype(BF16)


_POS_PAIRS = L_CMP // 2


def _gelu_tanh(x):
    return x * (0.5 * (1.0 + jnp.tanh(GELU_C * (x + 0.044715 * (x * x * x)))))


def _compress_kernel(xe_ref, xo_ref, wc_ref, b1_ref, w2_ref, kc_ref, vc_ref, acc_ref):
    pp = pl.program_id(1)

    @pl.when(pp == 0)
    def _():
        acc_ref[...] = jnp.zeros_like(acc_ref)

    for kv in range(2):
        for gh in range(2):
            c0 = kv * KV_WIDTH + gh * LANE
            chunk = jnp.concatenate([xe_ref[:, c0:c0 + LANE], xo_ref[:, c0:c0 + LANE]], axis=1)
            a0 = (kv * 2 + gh) * 256
            acc_ref[:, a0:a0 + 256] += _dot(chunk.astype(BF16), wc_ref[kv])

    @pl.when(pp == _POS_PAIRS - 1)
    def _():
        for kv, o_ref in ((0, kc_ref), (1, vc_ref)):
            halves = []
            for gh in range(2):
                a0 = (kv * 2 + gh) * 256
                h = _gelu_tanh(acc_ref[:, a0:a0 + 256] + b1_ref[kv])
                halves.append(_dot(h.astype(BF16), w2_ref[kv]))
            o_ref[...] = jnp.concatenate(halves, axis=1)


def _compress_weights(w1, b1, w2):
    eye2 = jnp.eye(2, dtype=F32)
    w1r = w1.reshape(2, _POS_PAIRS, 2, HEAD_DIM, CMP_HIDDEN)
    wc = jnp.einsum("kpidh,gG->kpigdGh", w1r, eye2).reshape(2, _POS_PAIRS, 256, 256).astype(BF16)
    b1t = jnp.concatenate([b1, b1], axis=-1).reshape(2, 1, 256)
    w2e = jnp.einsum("khd,gG->kghGd", w2, eye2).reshape(2, 256, LANE).astype(BF16)
    return wc, b1t, w2e


def _compress(x2, wc, b1t, w2e, r):
    nb = x2.shape[0]
    blk = lambda off: pl.BlockSpec((r, 2 * KV_WIDTH), lambda i, pp: (i, 2 * pp + off))
    out = pl.BlockSpec((r, KV_WIDTH), lambda i, pp: (i, 0))
    return pl.pallas_call(
        _compress_kernel,
        grid=(nb // r, _POS_PAIRS),
        in_specs=[blk(0), blk(1),
                  pl.BlockSpec((2, None, 256, 256), lambda i, pp: (0, pp, 0, 0)),
                  _const_spec((2, 1, 256)), _const_spec((2, 256, LANE))],
        out_specs=[out, out],
        out_shape=[jax.ShapeDtypeStruct((nb, KV_WIDTH), F32)] * 2,
        scratch_shapes=[pltpu.VMEM((r, 1024), F32)],
        compiler_params=_cparams(("parallel", "arbitrary")),
        name="compress",
    )(x2, x2, wc, b1t, w2e)


_PAGE = 128
_PP = 32
_BPP = _PAGE // L_CMP


def _compress_pool_kernel(x_ref, wc_ref, b1_ref, w2_ref, o_ref, xs_ref):
    def relayout(j, carry):
        r0 = pl.multiple_of(j * _PAGE, _PAGE)
        for kv in range(2):
            x = x_ref[j, kv].T
            for gh in range(2):
                xs_ref[kv * 2 + gh, pl.ds(r0, _PAGE), :] = x[:, gh * LANE:(gh + 1) * LANE]
        return carry

    lax.fori_loop(0, _PP, relayout, 0)
    nblk = _PP * _BPP
    for kv in range(2):
        for gh in range(2):
            acc = jnp.zeros((nblk, 256), F32)
            for pp in range(_POS_PAIRS):
                a0 = xs_ref[kv * 2 + gh, pl.ds(2 * pp, nblk, stride=L_CMP), :]
                a1 = xs_ref[kv * 2 + gh, pl.ds(2 * pp + 1, nblk, stride=L_CMP), :]
                chunk = jnp.concatenate([a0, a1], axis=1).astype(BF16)
                acc = acc + _dot(chunk, wc_ref[kv, pp])
            h = _gelu_tanh(acc + b1_ref[kv])
            c0 = kv * KV_WIDTH + gh * LANE
            o_ref[:, c0:c0 + LANE] = _dot(h.astype(BF16), w2_ref[kv])


def _compress_pool(cache_t, layer, wc, b1t, w2e):
    n_pool = cache_t.shape[1]
    return pl.pallas_call(
        _compress_pool_kernel,
        grid=(n_pool // _PP,),
        in_specs=[pl.BlockSpec((None, _PP, 2, KV_WIDTH, _PAGE), lambda i: (layer, i, 0, 0, 0)),
                  _const_spec((2, _POS_PAIRS, 256, 256)), _const_spec((2, 1, 256)),
                  _const_spec((2, 256, LANE))],
        out_specs=pl.BlockSpec((_PP * _BPP, 2 * KV_WIDTH), lambda i: (i, 0)),
        out_shape=jax.ShapeDtypeStruct((n_pool * _BPP, 2 * KV_WIDTH), F32),
        scratch_shapes=[pltpu.VMEM((4, _PP * _PAGE, LANE), F32)],
        compiler_params=_cparams(("parallel",)),
        name="compress_pool",
    )(cache_t, wc, b1t, w2e)


def _topk_member(score, n_rows):
    sb = lax.broadcasted_iota(jnp.int32, score.shape, 0)
    cnt = jnp.zeros(score.shape, jnp.int32)
    for jj in range(n_rows):
        row = score[jj:jj + 1, :]
        beats = (row > score) | ((row == score) & (jj < sb))
        cnt = cnt + jnp.where(beats, 1, 0)
    return cnt < N_SEL


_TQ = 256


def _select_kernel(q_ref, kcp_ref, qa_ref):
    t0 = pl.program_id(2) * _TQ
    qpos = t0 + lax.broadcasted_iota(jnp.int32, (1, _TQ), 1)
    r = lax.broadcasted_iota(jnp.int32, (2 * 32, 1), 0)
    blk = jnp.where(r < 32, 2 * r, 2 * (r - 32) + 1)
    mask_c = ((blk + 1) * L_CMP - 1) <= qpos
    q = q_ref[...]
    kcp = kcp_ref[...]
    p_slc = jnp.zeros((32, _TQ), F32)
    for j in range(GROUP):
        s = _dot_nt(kcp, q[:, j * HEAD_DIM:(j + 1) * HEAD_DIM])
        s = jnp.where(mask_c, s, -jnp.inf)
        m = jnp.max(s, axis=0, keepdims=True)
        m = jnp.where(m == -jnp.inf, 0.0, m)
        p = jnp.exp(s - m)
        d = jnp.sum(p, axis=0, keepdims=True)
        p = p / jnp.where(d > 0, d, 1.0)
        p_slc = p_slc + (p[:32] + p[32:])
    sb = lax.broadcasted_iota(jnp.int32, (32, 1), 0)
    cur = qpos // L_SLC
    forced = (sb == 0) | (sb == cur) | (sb == cur - 1)
    score = p_slc + jnp.where(forced, FORCE_BONUS, 0.0)
    score = jnp.where(sb * L_SLC <= qpos, score, NEG_SCORE)
    sel = _topk_member(score, 32)
    bias_t = jnp.where(sel, 0.0, MASK_BIAS)
    bias = jnp.concatenate([bias_t, jnp.zeros((LANE - 32, _TQ), F32)], axis=0).T
    for j in range(GROUP):
        qh = q[:, j * HEAD_DIM:(j + 1) * HEAD_DIM].astype(F32)
        qa_ref[j] = jnp.concatenate([qh, bias[:, :LANE - HEAD_DIM]], axis=1).astype(BF16)


def _select(q, kcp, b, t):
    return pl.pallas_call(
        _select_kernel,
        grid=(b, N_KV, t // _TQ),
        in_specs=[pl.BlockSpec((None, _TQ, KV_WIDTH), lambda bi, g, i: (bi, i, g)),
                  pl.BlockSpec((None, None, 64, HEAD_DIM), lambda bi, g, i: (bi, g, 0, 0))],
        out_specs=pl.BlockSpec((None, None, GROUP, _TQ, LANE), lambda bi, g, i: (bi, g, 0, i, 0)),
        out_shape=jax.ShapeDtypeStruct((b, N_KV, GROUP, t, LANE), BF16),
        compiler_params=_cparams(("parallel", "parallel", "parallel")),
        name="select",
    )(q, kcp)


_TK = 256
_ROWS = GROUP * _TQ
_ONES_ROW = HEAD_DIM


def _two_pass_attention(q, qi, row_t, k_scr, v_scr, n_tiles, far_window_mask, s_all, m_ref, acc_ref):
    kiota = lax.broadcasted_iota(jnp.int32, (1, _TK), 1)

    def scores(i, mask):
        kt = qi - i
        s = _dot(q, k_scr[kt])
        if mask is not None:
            s = jnp.where(mask(row_t - (kt * _TK + kiota)), s, -jnp.inf)
        s_all[i] = s
        return jnp.maximum(s[:, :LANE], s[:, LANE:])

    m_ref[...] = scores(0, lambda dist: dist >= 0)

    def pass1(i, carry):
        mask = (lambda dist: dist < WINDOW) if far_window_mask else None
        m_ref[...] = jnp.maximum(m_ref[...], scores(i, mask))
        return carry

    lax.fori_loop(1, n_tiles, pass1, 0)
    m_ref[...] = jnp.broadcast_to(jnp.max(m_ref[...], axis=-1, keepdims=True), m_ref.shape)
    acc_ref[...] = jnp.zeros(acc_ref.shape, F32)

    def pass2(i, carry):
        s = s_all[i]
        m = m_ref[...]
        p = jnp.exp(jnp.concatenate([s[:, :LANE] - m, s[:, LANE:] - m], axis=1)).astype(BF16)
        acc_ref[...] += _dot_nt(p, v_scr[qi - i])
        return carry

    lax.fori_loop(0, n_tiles, pass2, 0)
    acc = acc_ref[...]
    return acc[:, :HEAD_DIM] / acc[:, _ONES_ROW:_ONES_ROW + 1]


def _attn_kernel(qa_ref, ks_ref, vs_ref, kw_ref, vw_ref, kc_ref, vc_ref, g_ref, o_ref,
                 ks_scr, vs_scr, kw_scr, vw_scr, s_all, m_ref, acc_ref):
    qi = pl.program_id(2)
    n_kt = ks_scr.shape[0]

    @pl.when(qi == 0)
    def _():
        blk_row = lax.broadcasted_iota(jnp.int32, (32, _TK), 0)
        ones_row = jnp.where(lax.broadcasted_iota(jnp.int32, (16, _TK), 0) == 0, 1.0, 0.0).astype(BF16)
        for kt in range(n_kt):
            cols = slice(kt * _TK, (kt + 1) * _TK)
            lane_blk = (kt * _TK + lax.broadcasted_iota(jnp.int32, (32, _TK), 1)) // L_SLC
            ks_scr[kt, 0:HEAD_DIM, :] = ks_ref[:, cols].astype(BF16)
            ks_scr[kt, HEAD_DIM:HEAD_DIM + 32, :] = jnp.where(lane_blk == blk_row, 1.0, 0.0).astype(BF16)
            ks_scr[kt, HEAD_DIM + 32:, :] = jnp.zeros((LANE - HEAD_DIM - 32, _TK), BF16)
            kw_scr[kt, 0:HEAD_DIM, :] = kw_ref[:, cols].astype(BF16)
            kw_scr[kt, HEAD_DIM:, :] = jnp.zeros((LANE - HEAD_DIM, _TK), BF16)
            for v_ref, v_scr in ((vs_ref, vs_scr), (vw_ref, vw_scr)):
                v_scr[kt, 0:HEAD_DIM, :] = v_ref[:, cols].astype(BF16)
                v_scr[kt, HEAD_DIM:HEAD_DIM + 16, :] = ones_row
                v_scr[kt, HEAD_DIM + 16:, :] = jnp.zeros((LANE - HEAD_DIM - 16, _TK), BF16)

    t0 = qi * _TQ
    q = qa_ref[...].reshape(_ROWS, LANE)
    row_t = t0 + (lax.broadcasted_iota(jnp.int32, (_ROWS, 1), 0) & (_TQ - 1))

    s = _dot(q, kc_ref[...])
    cb_end = (lax.broadcasted_iota(jnp.int32, (1, 64), 1) + 1) * L_CMP - 1
    s = jnp.where(cb_end <= row_t, s, -jnp.inf)
    m = jnp.max(s, axis=-1, keepdims=True)
    m = jnp.where(m == -jnp.inf, 0.0, m)
    oc = _dot_nt(jnp.exp(s - m).astype(BF16), vc_ref[...])
    den = oc[:, _ONES_ROW:_ONES_ROW + 1]
    o_cmp = oc[:, :HEAD_DIM] / jnp.where(den > 0, den, 1.0)

    o_slc = _two_pass_attention(q, qi, row_t, ks_scr, vs_scr, qi + 1, False, s_all, m_ref, acc_ref)
    o_win = _two_pass_attention(q, qi, row_t, kw_scr, vw_scr, jnp.minimum(qi, WINDOW // _TK) + 1,
                                True, s_all, m_ref, acc_ref)

    g = g_ref[...]
    heads = []
    for j in range(GROUP):
        rs = slice(j * _TQ, (j + 1) * _TQ)
        c = N_BRANCH * j
        heads.append(g[:, c:c + 1] * o_cmp[rs] + g[:, c + 1:c + 2] * o_slc[rs]
                     + g[:, c + 2:c + 3] * o_win[rs])
    o_ref[...] = jnp.concatenate(heads, axis=1)


def _attn(qa, kvt, kc, vc, gates, b, t):
    head_t = lambda c: pl.BlockSpec((None, None, HEAD_DIM, t), lambda bi, g, i: (bi, N_KV * c + g, 0, 0))
    summ = pl.BlockSpec((None, None, LANE, 64), lambda bi, g, i: (bi, g, 0, 0))
    n_kt = t // _TK
    tiles = pltpu.VMEM((n_kt, LANE, _TK), BF16)
    return pl.pallas_call(
        _attn_kernel,
        grid=(b, N_KV, t // _TQ),
        in_specs=[pl.BlockSpec((None, None, GROUP, _TQ, LANE), lambda bi, g, i: (bi, g, 0, i, 0)),
                  head_t(2), head_t(3), head_t(4), head_t(5), summ, summ,
                  pl.BlockSpec((None, None, _TQ, LANE), lambda bi, g, i: (bi, g, i, 0))],
        out_specs=pl.BlockSpec((None, _TQ, KV_WIDTH), lambda bi, g, i: (bi, i, g)),
        out_shape=jax.ShapeDtypeStruct((b, t, D_MODEL), F32),
        scratch_shapes=[tiles, tiles, tiles, tiles,
                        pltpu.VMEM((n_kt, _ROWS, _TK), F32),
                        pltpu.VMEM((_ROWS, LANE), F32), pltpu.VMEM((_ROWS, LANE), F32)],
        compiler_params=_cparams(("arbitrary", "arbitrary", "arbitrary")),
        name="attn",
    )(qa, kvt, kvt, kvt, kvt, kc, vc, gates)


_N_PAGES = 16
_PAST = _N_PAGES * _PAGE
_SROWS = GROUP * N_KV * 4
_NEWPAD = 128
_NSB_PAD = 40
_NEWROWS = 8


def _diag_heads(o, row_g):
    col_g = lax.broadcasted_iota(jnp.int32, (1, KV_WIDTH), 1) // HEAD_DIM
    om = jnp.where(col_g == row_g, o, 0.0)
    return (om[:, 0:64] + om[:, 64:128]) + (om[:, 128:192] + om[:, 192:256])


def _softmax_rows(s):
    m = jnp.max(s, axis=-1, keepdims=True)
    m = jnp.where(m == -jnp.inf, 0.0, m)
    p = jnp.exp(s - m)
    d = jnp.sum(p, axis=-1, keepdims=True)
    return p / jnp.where(d > 0, d, 1.0)


def _softmax_two(s_old, s_new):
    m = jnp.maximum(jnp.max(s_old, axis=-1, keepdims=True), jnp.max(s_new, axis=-1, keepdims=True))
    m = jnp.where(m == -jnp.inf, 0.0, m)
    p_old = jnp.exp(s_old - m)
    p_new = jnp.exp(s_new - m)
    d = jnp.sum(p_old, axis=-1, keepdims=True) + jnp.sum(p_new, axis=-1, keepdims=True)
    inv = 1.0 / jnp.where(d > 0, d, 1.0)
    return (p_old * inv).astype(BF16), (p_new * inv).astype(BF16)


def _sample_attn_kernel(dt, pt_ref, qbd_ref, *refs):
    del pt_ref
    page_refs = refs[:_N_PAGES]
    sum_refs = refs[_N_PAGES:2 * _N_PAGES]
    kvn_ref, wold_ref, wnewt_ref, g_ref, oh_ref, o_ref, wout_ref, sum_scr, kcp_scr = refs[2 * _N_PAGES:]

    qbd = qbd_ref[...]
    rows = lax.broadcasted_iota(jnp.int32, (_SROWS, 1), 0)
    row_t = rows % dt
    row_g = (rows // dt) % N_KV

    per_page = _PAGE // L_CMP
    for j in range(_N_PAGES):
        sum_scr[per_page * j:per_page * (j + 1), :] = sum_refs[j][...]
        for i in range(per_page):
            dst = (i % 2) * 32 + (per_page * j + i) // 2
            kcp_scr[dst:dst + 1, :] = sum_refs[j][i:i + 1, :KV_WIDTH]
    kvc = sum_scr[...]
    p = _softmax_rows(_dot_nt(qbd, kvc[:, :KV_WIDTH].astype(BF16)))
    o_cmp = _diag_heads(_dot(p.astype(BF16), kvc[:, KV_WIDTH:].astype(BF16)), row_g)

    s_t = _dot_nt(kcp_scr[...].astype(BF16), qbd)
    m = jnp.max(s_t, axis=0, keepdims=True)
    pt = jnp.exp(s_t - m)
    pt = pt / jnp.sum(pt, axis=0, keepdims=True)
    pp = pt[:32] + pt[32:]
    nq = N_KV * dt
    p_slc = (pp[:, 0:nq] + pp[:, nq:2 * nq]) + (pp[:, 2 * nq:3 * nq] + pp[:, 3 * nq:4 * nq])
    p_slc = jnp.concatenate([p_slc, jnp.zeros((_NSB_PAD - 32, nq), F32)], axis=0)
    sb = lax.broadcasted_iota(jnp.int32, (_NSB_PAD, 1), 0)
    qpos = _PAST + lax.broadcasted_iota(jnp.int32, (1, nq), 1) % dt
    cur = qpos // L_SLC
    forced = (sb == 0) | (sb == cur) | (sb == cur - 1)
    score = p_slc + jnp.where(forced, FORCE_BONUS, 0.0)
    score = jnp.where(sb * L_SLC <= qpos, score, NEG_SCORE)
    score = jnp.where(sb < _PAST // L_SLC + 1, score, -jnp.inf)
    sel_t = _topk_member(score, _PAST // L_SLC + 1)
    bias_t = jnp.where(sel_t, 0.0, MASK_BIAS)
    bias_t = jnp.concatenate([bias_t, jnp.zeros((LANE - _NSB_PAD, nq), F32)], axis=0)
    bias_t = jnp.concatenate([bias_t, jnp.zeros((LANE, LANE - nq), F32)], axis=1)
    bias = bias_t.T[:nq]
    bias = jnp.concatenate([bias] * GROUP, axis=0).astype(BF16)

    knew = kvn_ref[...]
    zpad = jnp.zeros((_NEWPAD - _NEWROWS, KV_WIDTH), F32)
    new_tm = lambda c: jnp.concatenate([knew[:, c * KV_WIDTH:(c + 1) * KV_WIDTH], zpad], axis=0).astype(BF16)
    nidx = lax.broadcasted_iota(jnp.int32, (1, _NEWPAD), 1)
    kt_all = jnp.concatenate([r[0] for r in page_refs], axis=1).astype(BF16)
    vt_all = jnp.concatenate([r[1] for r in page_refs], axis=1).astype(BF16)
    oh = oh_ref[...]
    s_old = _dot(qbd, kt_all) + _dot(bias, oh[:, :_PAST])
    s_new = _dot_nt(qbd, new_tm(2)) + _dot(bias, oh[:, _PAST:])
    s_new = jnp.where(nidx <= row_t, s_new, -jnp.inf)
    p_old, p_new = _softmax_two(s_old, s_new)
    o_slc = _diag_heads(_dot_nt(p_old, vt_all) + _dot(p_new, new_tm(3)), row_g)

    wb = wold_ref.shape[2]
    s_old = _dot(qbd, wold_ref[0].astype(BF16))
    widx = lax.broadcasted_iota(jnp.int32, (1, wb), 1)
    dist = row_t + (wb - widx)
    s_old = jnp.where((dist >= 0) & (dist < WINDOW), s_old, -jnp.inf)
    s_new = _dot_nt(qbd, new_tm(4))
    dist = row_t - nidx
    s_new = jnp.where((dist >= 0) & (dist < WINDOW), s_new, -jnp.inf)
    p_old, p_new = _softmax_two(s_old, s_new)
    o_win = _diag_heads(_dot_nt(p_old, wold_ref[1].astype(BF16)) + _dot(p_new, new_tm(5)), row_g)

    g = g_ref[...]
    o_ref[...] = g[:, 0:1] * o_cmp + g[:, 1:2] * o_slc + g[:, 2:3] * o_win

    lane = lax.broadcasted_iota(jnp.int32, (1, LANE), 1)
    for c in range(2):
        shifted = pltpu.roll(wold_ref[c], wb - dt, axis=1)
        wout_ref[c, :, 0:wb - LANE] = shifted[:, 0:wb - LANE]
        wout_ref[c, :, wb - LANE:wb] = jnp.where(lane >= LANE - dt, wnewt_ref[c], shifted[:, wb - LANE:wb])


def _sample_attn(page_table, qbd, cache_t, layer, kvc_pool, kvn, win_t, wnewt, grow, onehot, dt):
    nb = qbd.shape[0]
    wb = win_t.shape[-1]
    kvn = jnp.pad(kvn, ((0, 0), (0, _NEWROWS - dt), (0, 0)))
    page = lambda j: pl.BlockSpec((None, None, 2, KV_WIDTH, _PAGE),
                                  lambda b, pt: (layer, pt[b * _N_PAGES + j], 1, 0, 0))
    summ = lambda j: pl.BlockSpec((None, _BPP, 2 * KV_WIDTH),
                                  lambda b, pt: (pt[b * _N_PAGES + j], 0, 0))
    per_seq = lambda *shape: pl.BlockSpec((None,) + shape, lambda b, pt: (b,) + (0,) * len(shape))
    grid_spec = pltpu.PrefetchScalarGridSpec(
        num_scalar_prefetch=1,
        grid=(nb,),
        in_specs=[per_seq(_SROWS, KV_WIDTH)]
        + [page(j) for j in range(_N_PAGES)] + [summ(j) for j in range(_N_PAGES)]
        + [per_seq(_NEWROWS, _KV6),
           pl.BlockSpec((None, None, 2, KV_WIDTH, wb), lambda b, pt: (layer, b, 0, 0, 0)),
           per_seq(2, KV_WIDTH, LANE), per_seq(_SROWS, LANE),
           pl.BlockSpec((LANE, _PAST + _NEWPAD), lambda b, pt: (0, 0))],
        out_specs=[per_seq(_SROWS, HEAD_DIM), per_seq(2, KV_WIDTH, wb)],
        scratch_shapes=[pltpu.VMEM((_BPP * _N_PAGES, 2 * KV_WIDTH), F32),
                        pltpu.VMEM((_BPP * _N_PAGES, KV_WIDTH), F32)],
    )
    return pl.pallas_call(
        functools.partial(_sample_attn_kernel, dt),
        grid_spec=grid_spec,
        out_shape=[jax.ShapeDtypeStruct((nb, _SROWS, HEAD_DIM), F32),
                   jax.ShapeDtypeStruct((nb, 2, KV_WIDTH, wb), F32)],
        compiler_params=_cparams(("parallel",)),
        name="sample_attn",
    )(page_table.reshape(-1), qbd, *([cache_t] * _N_PAGES), *([kvc_pool] * _N_PAGES),
      kvn, win_t, wnewt, grow, onehot)


_TT = 256
_HALO = 32
_RC = 64


def _dwconv_kernel(cur_ref, prev_ref, w_ref, b_ref, y_ref, h_ref):
    prev = prev_ref[...]
    h_ref[0:_HALO, :] = jnp.where(pl.program_id(1) == 0, 0.0, prev)
    h_ref[_HALO:, :] = cur_ref[...]
    off = _HALO - (CONV_W - 1)
    for r0 in range(0, _TT, _RC):
        acc = jnp.broadcast_to(b_ref[...], (_RC, C_CONV))
        for k in range(CONV_W):
            acc = acc + w_ref[k:k + 1, :] * h_ref[r0 + off + k:r0 + off + k + _RC, :]
        y_ref[r0:r0 + _RC, :] = acc


def _dwconv(conv_in, w_pad, b_dw):
    b, t, _ = conv_in.shape
    return pl.pallas_call(
        _dwconv_kernel,
        grid=(b, t // _TT),
        in_specs=[pl.BlockSpec((None, _TT, C_CONV), lambda bi, i: (bi, i, 0)),
                  pl.BlockSpec((None, _HALO, C_CONV),
                               lambda bi, i: (bi, jnp.maximum(i * (_TT // _HALO) - 1, 0), 0)),
                  _const_spec((_HALO, C_CONV)), _const_spec((1, C_CONV))],
        out_specs=pl.BlockSpec((None, _TT, C_CONV), lambda bi, i: (bi, i, 0)),
        out_shape=jax.ShapeDtypeStruct((b, t, C_CONV), F32),
        scratch_shapes=[pltpu.VMEM((_TT + _HALO, C_CONV), F32)],
        compiler_params=_cparams(("parallel", "parallel")),
        name="dwconv",
    )(conv_in, conv_in, w_pad, b_dw)


_SB = 8


def _dwconv_sample_kernel(h_ref, w_ref, b_ref, y_ref):
    dt = y_ref.shape[0]
    w = w_ref[0:CONV_W, :]
    for t in range(dt):
        y_ref[t:t + 1, :] = jnp.sum(h_ref[t:t + CONV_W, :] * w, axis=0, keepdims=True) + b_ref[...]


def _dwconv_sample(h2, w_pad, b_dw, b, c):
    rows = h2.shape[0]
    dt = rows - (CONV_W - 1)
    w_t = jnp.tile(w_pad, (1, _SB))
    b_t = jnp.tile(b_dw, (1, _SB))
    return pl.pallas_call(
        _dwconv_sample_kernel,
        grid=(b // _SB,),
        in_specs=[pl.BlockSpec((rows, _SB * c), lambda i: (0, i)),
                  _const_spec((_HALO, _SB * c)), _const_spec((1, _SB * c))],
        out_specs=pl.BlockSpec((dt, _SB * c), lambda i: (0, i)),
        out_shape=jax.ShapeDtypeStruct((dt, b * c), F32),
        compiler_params=_cparams(("parallel",)),
        name="dwconv_sample",
    )(h2, w_t, b_t)


def _mix_out_kernel(y_ref, lng_ref, lnb_ref, wpw_ref, on_ref, ga_ref, gb_ref, wo_ref,
                    x_ref, gt_ref, gain_ref, o_ref):
    y = y_ref[...]
    yc = y - jnp.mean(y, axis=-1, keepdims=True)
    var = jnp.mean(yc * yc, axis=-1, keepdims=True)
    yn = yc * lax.rsqrt(var + EPS) * lng_ref[...] + lnb_ref[...]
    o_conv = _dot((yn * _sigmoid(yn)).astype(BF16), wpw_ref[...])
    merged = ga_ref[...] * on_ref[...] + gb_ref[...] * o_conv
    yo = _dot(merged.astype(BF16), wo_ref[...])
    o_ref[...] = x_ref[...] + gt_ref[...] * _rms(yo, gain_ref[...])


def _mix_out(ydw, lng, lnb, wpw, o_nsa, ga, gb, wo, x, gt, gain, tm, rows_per_batch):
    n = x.shape[0]
    row = lambda c: _row_spec(tm, c)
    return pl.pallas_call(
        _mix_out_kernel,
        grid=(n // tm,),
        in_specs=[row(C_CONV), _const_spec((1, C_CONV)), _const_spec((1, C_CONV)),
                  _const_spec((C_CONV, D_MODEL)), row(D_MODEL), row(D_MODEL), row(D_MODEL),
                  _const_spec((D_MODEL, D_MODEL)), row(D_MODEL), _mod_spec(gt, tm, rows_per_batch),
                  _const_spec((1, D_MODEL))],
        out_specs=row(D_MODEL),
        out_shape=jax.ShapeDtypeStruct((n, D_MODEL), F32),
        compiler_params=_cparams(("parallel",)),
        name="mix_out",
    )(ydw, lng, lnb, wpw, o_nsa, ga, gb, wo, x, gt, gain)


_FC = D_FFN // 2


def _ffn_kernel(x_ref, sc_ref, sh_ref, gt_ref, g2_ref, g3_ref, wi_ref, wo_ref, o_ref):
    x = x_ref[...]
    h = (_rms(x, g2_ref[...]) * (1.0 + sc_ref[...]) + sh_ref[...]).astype(BF16)
    y = jnp.zeros(x.shape, F32)
    for c in range(0, D_FFN, _FC):
        a = _dot(h, wi_ref[:, c:c + _FC])
        b = _dot(h, wi_ref[:, D_FFN + c:D_FFN + c + _FC])
        u = (a * _sigmoid(a) * b).astype(BF16)
        y = y + _dot(u, wo_ref[c:c + _FC, :])
    o_ref[...] = x + gt_ref[...] * _rms(y, g3_ref[...])


def _ffn(x, sc, sh, gt, g2, g3, wi, wo, tm, rows_per_batch):
    n = x.shape[0]
    row = _row_spec(tm, D_MODEL)
    mod = lambda a: _mod_spec(a, tm, rows_per_batch)
    return pl.pallas_call(
        _ffn_kernel,
        grid=(n // tm,),
        in_specs=[row, mod(sc), mod(sh), mod(gt), _const_spec((1, D_MODEL)), _const_spec((1, D_MODEL)),
                  _const_spec((D_MODEL, 2 * D_FFN)), _const_spec((D_FFN, D_MODEL))],
        out_specs=row,
        out_shape=jax.ShapeDtypeStruct((n, D_MODEL), F32),
        compiler_params=_cparams(("parallel",)),
        name="ffn",
    )(x, sc, sh, gt, g2, g3, wi, wo)


def _summaries_t(s, b, n_cb, ones_row):
    st = s.reshape(b, n_cb, N_KV, HEAD_DIM).transpose(0, 2, 3, 1)
    extra = jnp.zeros((b, N_KV, LANE - HEAD_DIM, n_cb), F32)
    if ones_row:
        extra = extra.at[:, :, 0, :].set(1.0)
    return jnp.concatenate([st, extra], axis=2).astype(BF16)


def _prompt_mixer(q, kvt, gates, kc, vc, b, t):
    n_cb = t // L_CMP
    kch = kc.reshape(b, n_cb, N_KV, HEAD_DIM).transpose(0, 2, 1, 3).astype(BF16)
    kcp = jnp.concatenate([kch[:, :, 0::2], kch[:, :, 1::2]], axis=2)
    g4 = gates[:, :N_HEADS * N_BRANCH].reshape(b, t, N_KV, GROUP * N_BRANCH).transpose(0, 2, 1, 3)
    g4 = jnp.concatenate([g4, jnp.zeros((b, N_KV, t, LANE - GROUP * N_BRANCH), F32)], axis=-1)
    qa = _select(q.reshape(b, t, D_MODEL), kcp, b, t)
    o = _attn(qa, kvt.reshape(b, 6 * N_KV, HEAD_DIM, t), _summaries_t(kc, b, n_cb, False),
              _summaries_t(vc, b, n_cb, True), g4, b, t)
    return o.reshape(b * t, D_MODEL)


def _sample_mixer(q, kv, gates, kvc_pool, cache_t, layer, page_table, win_t, b, dt):
    qr = q.reshape(b, dt, N_KV, GROUP, HEAD_DIM).transpose(0, 3, 2, 1, 4)
    eye = jnp.eye(N_KV, dtype=BF16)
    qbd = (qr[:, :, :, :, None, :] * eye[None, None, :, None, :, None]).reshape(b, _SROWS, KV_WIDTH)
    gr = gates[:, :N_HEADS * N_BRANCH].reshape(b, dt, N_KV, GROUP, N_BRANCH).transpose(0, 3, 2, 1, 4)
    gr = gr.reshape(b, _SROWS, N_BRANCH)
    gr = jnp.concatenate([gr, jnp.zeros((b, _SROWS, LANE - N_BRANCH), F32)], axis=-1)
    nkeys = _PAST + _NEWPAD
    onehot = (jnp.arange(nkeys)[None, :] // L_SLC == jnp.arange(LANE)[:, None]).astype(BF16)
    kv3 = kv.reshape(b, dt, _KV6)
    wnewt = kv3[:, :, 4 * KV_WIDTH:].transpose(0, 2, 1).reshape(b, 2, KV_WIDTH, dt)
    wnewt = jnp.pad(wnewt, ((0, 0), (0, 0), (0, 0), (LANE - dt, 0)))
    o, wout = _sample_attn(page_table, qbd, cache_t, layer, kvc_pool, kv3, win_t, wnewt, gr, onehot, dt)
    o = o.reshape(b, GROUP, N_KV, dt, HEAD_DIM).transpose(0, 3, 2, 1, 4)
    return o.reshape(b * dt, D_MODEL), wout


def _feature_major(a):
    lead = a.shape[:-4]
    rows, p = a.shape[-4], a.shape[-3]
    nd = len(lead)
    perm = tuple(range(nd)) + (nd + 1, nd + 2, nd + 3, nd)
    return a.transpose(perm).reshape(lead + (p, KV_WIDTH, rows))


def _token_major(a_t, p):
    b, _, rows = a_t.shape
    return a_t.reshape(b, p, N_KV, HEAD_DIM, rows).transpose(0, 4, 1, 2, 3)


def kernel(x_prompt, x_sample, cache_kv, state_win, state_conv, page_table, c_prompt, c_sample,
           w_ada, b_ada, norm_gain, w_in, w_cmp1, b_cmp1, w_cmp2, w_dw, b_dw, ln_conv_g,
           ln_conv_b, w_pw2, w_out, w_ffn_in, w_ffn_out):
    bp, t, _ = x_prompt.shape
    bs, dt, _ = x_sample.shape
    n_pool = cache_kv.shape[1]
    tm_p, tm_s = 256, 256

    ada = _ada(jnp.concatenate([c_prompt, c_sample], axis=0), w_ada, b_ada)
    cache_t = _feature_major(cache_kv)
    win_t = _feature_major(state_win)
    conv_t = state_conv.transpose(0, 2, 1, 3)

    xp = x_prompt.reshape(bp * t, D_MODEL)
    xs = x_sample.reshape(bs * dt, D_MODEL)
    kv_p, kv_s, win_p, win_s, conv_p, conv_s = [], [], [], [], [], []
    for l in range(DEPTH):
        mods_p = [m.reshape(bp, 1, D_MODEL) for m in jnp.split(ada[l, :bp], 6, axis=-1)]
        mods_s = [jnp.repeat(m, dt, axis=0) for m in jnp.split(ada[l, bp:], 6, axis=-1)]
        gain = norm_gain[l].reshape(4, 1, D_MODEL)
        wproj, wkv_t = _proj_weight(w_in[l])
        wc, b1t, w2e = _compress_weights(w_cmp1[l], b_cmp1[l], w_cmp2[l])
        w_pad = jnp.concatenate([w_dw[l], jnp.zeros((_HALO - CONV_W, C_CONV), F32)], axis=0)
        bdw = b_dw[l].reshape(1, C_CONV)
        lng = ln_conv_g[l].reshape(1, C_CONV)
        lnb = ln_conv_b[l].reshape(1, C_CONV)
        wpw = w_pw2[l].astype(BF16)
        wo = w_out[l].astype(BF16)
        wfi = w_ffn_in[l].astype(BF16)
        wfo = w_ffn_out[l].astype(BF16)

        sh_m, sc_m, gt_m, sh_f, sc_f, gt_f = mods_p
        q, kvt, cmp_tm, conv_in, ga, gb, gates = _proj_prompt(xp, sc_m, sh_m, gain[0], wproj, wkv_t,
                                                              tm_p, bp, t)
        n_blk = bp * t // L_CMP
        kc, vc = _compress(cmp_tm.reshape(n_blk, L_CMP * 2 * KV_WIDTH), wc, b1t, w2e, n_blk)
        o_nsa = _prompt_mixer(q, kvt, gates, kc, vc, bp, t)
        conv_in3 = conv_in.reshape(bp, t, C_CONV)
        ydw = _dwconv(conv_in3, w_pad, bdw).reshape(bp * t, C_CONV)
        xp = _mix_out(ydw, lng, lnb, wpw, o_nsa, ga, gb, wo, xp, gt_m, gain[1], tm_p, t)
        xp = _ffn(xp, sc_f, sh_f, gt_f, gain[2], gain[3], wfi, wfo, tm_p, t)
        kv_p.append(_token_major(kvt[:, :4 * KV_WIDTH], 4))
        win_p.append(_token_major(kvt[:, 4 * KV_WIDTH:, t - min(WINDOW, t):], 2))
        conv_p.append(conv_in3[:, t - (CONV_W - 1):])

        sh_m, sc_m, gt_m, sh_f, sc_f, gt_f = mods_s
        q, kv, conv_in, ga, gb, gates = _proj_sample(xs, sc_m, sh_m, gain[0], wproj, tm_s, dt)
        kvc_pool = _compress_pool(cache_t, l, wc, b1t, w2e).reshape(n_pool, _BPP, 2 * KV_WIDTH)
        o_nsa, wout = _sample_mixer(q, kv, gates, kvc_pool, cache_t, l, page_table, win_t, bs, dt)
        h2 = jnp.concatenate([conv_t[l], conv_in.reshape(bs, dt, C_CONV).transpose(1, 0, 2)], axis=0)
        ydw = _dwconv_sample(h2.reshape(CONV_W - 1 + dt, bs * C_CONV), w_pad, bdw, bs, C_CONV)
        ydw = ydw.reshape(dt, bs, C_CONV).transpose(1, 0, 2).reshape(bs * dt, C_CONV)
        xs = _mix_out(ydw, lng, lnb, wpw, o_nsa, ga, gb, wo, xs, gt_m, gain[1], tm_s, dt)
        xs = _ffn(xs, sc_f, sh_f, gt_f, gain[2], gain[3], wfi, wfo, tm_s, dt)
        kv_s.append(kv[:, :4 * KV_WIDTH].reshape(bs, dt, 4, N_KV, HEAD_DIM))
        win_s.append(_token_major(wout.reshape(bs, 2 * KV_WIDTH, -1), 2))
        conv_s.append(h2[dt:].transpose(1, 0, 2))

    return (xp.reshape(bp, t, D_MODEL), xs.reshape(bs, dt, D_MODEL),
            jnp.stack(kv_p), jnp.stack(kv_s), jnp.stack(win_p), jnp.stack(win_s),
            jnp.stack(conv_p), jnp.stack(conv_s))
```

```python
import functools

import jax
import jax.numpy as jnp
import numpy as np
from jax import lax
from jax.experimental import pallas as pl
from jax.experimental.pallas import tpu as pltpu

D_MODEL = 1024
DEPTH = 2
N_HEADS = 16
HEAD_DIM = 64
N_KV = 4
GROUP = 4
KV_WIDTH = N_KV * HEAD_DIM
L_CMP = 32
L_SLC = 64
N_SEL = 16
WINDOW = 512
N_BRANCH = 3
CMP_HIDDEN = 128
C_CONV = 512
CONV_W = 31
D_FFN = 2816
EPS = 1e-6
FORCE_BONUS = 1e4
NEG_SCORE = -1e9
ATTN_SCALE = HEAD_DIM ** -0.5

MASK_BIAS = -(2.0 ** 100)
GELU_C = float(np.float32(np.sqrt(2.0 / np.pi)))

LANE = 128
VMEM_LIMIT = 56 * 1024 * 1024

F32 = jnp.float32
BF16 = jnp.bfloat16

_OFF_Q = 0
_OFF_KV = 1024
_OFF_GLU = 2560
_OFF_MG = 3584
_OFF_G = 5632
_N_PROJ = 5760
_KV6 = 6 * KV_WIDTH


def _cparams(sem):
    return pltpu.CompilerParams(dimension_semantics=sem, vmem_limit_bytes=VMEM_LIMIT)


def _dot(a, b):
    return jnp.dot(a, b, preferred_element_type=F32)


def _dot_nt(a, b):
    return lax.dot_general(a, b, (((1,), (1,)), ((), ())), preferred_element_type=F32)


def _sigmoid(x):
    return 1.0 / (1.0 + jnp.exp(-x))


def _rms(x, g):
    return x * lax.rsqrt(jnp.mean(x * x, axis=-1, keepdims=True) + EPS) * g


def _const_spec(shape):
    nd = len(shape)
    return pl.BlockSpec(shape, lambda *_: (0,) * nd)


def _ada_kernel(c_ref, w_ref, b_ref, o_ref):
    c = c_ref[...]
    s = (c * _sigmoid(c)).astype(BF16)
    o_ref[...] = _dot(s, w_ref[...].astype(BF16)) + b_ref[...]


def _ada(c_all, w_ada, b_ada):
    n = c_all.shape[0]
    tn = 1536
    return pl.pallas_call(
        _ada_kernel,
        grid=(DEPTH, 6 * D_MODEL // tn),
        in_specs=[
            pl.BlockSpec((n, D_MODEL), lambda l, j: (0, 0)),
            pl.BlockSpec((None, D_MODEL, tn), lambda l, j: (l, 0, j)),
            pl.BlockSpec((None, 1, tn), lambda l, j: (l, 0, j)),
        ],
        out_specs=pl.BlockSpec((None, n, tn), lambda l, j: (l, 0, j)),
        out_shape=jax.ShapeDtypeStruct((DEPTH, n, 6 * D_MODEL), F32),
        compiler_params=_cparams(("parallel", "parallel")),
        name="ada",
    )(c_all, w_ada, b_ada.reshape(DEPTH, 1, 6 * D_MODEL))


def _proj_tail(h, w_ref, conv_ref, ga_ref, gb_ref, gate_ref):
    glu = _dot(h, w_ref[:, _OFF_GLU:_OFF_MG])
    conv_ref[...] = glu[:, :C_CONV] * _sigmoid(glu[:, C_CONV:])
    mg = _sigmoid(_dot(h, w_ref[:, _OFF_MG:_OFF_G]))
    ga_ref[...] = mg[:, :D_MODEL]
    gb_ref[...] = mg[:, D_MODEL:]
    gate_ref[...] = _sigmoid(_dot(h, w_ref[:, _OFF_G:_N_PROJ]))


def _proj_prompt_kernel(x_ref, sc_ref, sh_ref, g_ref, w_ref, wt_ref,
                        q_ref, kvt_ref, cmp_ref, conv_ref, ga_ref, gb_ref, gate_ref):
    h = (_rms(x_ref[...], g_ref[...]) * (1.0 + sc_ref[...]) + sh_ref[...]).astype(BF16)
    q_ref[...] = _dot(h, w_ref[:, _OFF_Q:_OFF_KV]).astype(BF16)
    kvt_ref[...] = _dot_nt(wt_ref[...], h)
    cmp_ref[...] = _dot(h, w_ref[:, _OFF_KV:_OFF_KV + 2 * KV_WIDTH])
    _proj_tail(h, w_ref, conv_ref, ga_ref, gb_ref, gate_ref)


def _proj_sample_kernel(x_ref, sc_ref, sh_ref, g_ref, w_ref,
                        q_ref, kv_ref, conv_ref, ga_ref, gb_ref, gate_ref):
    h = (_rms(x_ref[...], g_ref[...]) * (1.0 + sc_ref[...]) + sh_ref[...]).astype(BF16)
    q_ref[...] = _dot(h, w_ref[:, _OFF_Q:_OFF_KV]).astype(BF16)
    kv_ref[...] = _dot(h, w_ref[:, _OFF_KV:_OFF_GLU])
    _proj_tail(h, w_ref, conv_ref, ga_ref, gb_ref, gate_ref)


def _mod_spec(mod, tm, rows_per_batch):
    if mod.ndim == 3:
        return pl.BlockSpec((None, 1, D_MODEL), lambda i: (i * tm // rows_per_batch, 0, 0))
    return pl.BlockSpec((tm, D_MODEL), lambda i: (i, 0))


def _row_spec(tm, c):
    return pl.BlockSpec((tm, c), lambda i: (i, 0))


def _proj_tail_out(n, tm):
    specs = [_row_spec(tm, C_CONV), _row_spec(tm, D_MODEL), _row_spec(tm, D_MODEL), _row_spec(tm, LANE)]
    shapes = [jax.ShapeDtypeStruct((n, C_CONV), F32), jax.ShapeDtypeStruct((n, D_MODEL), F32),
              jax.ShapeDtypeStruct((n, D_MODEL), F32), jax.ShapeDtypeStruct((n, LANE), F32)]
    return specs, shapes


def _proj_prompt(x, sc, sh, gain, w, wt, tm, b, t):
    n = x.shape[0]
    per_b = t // tm
    tail_specs, tail_shapes = _proj_tail_out(n, tm)
    return pl.pallas_call(
        _proj_prompt_kernel,
        grid=(n // tm,),
        in_specs=[_row_spec(tm, D_MODEL), _mod_spec(sc, tm, t), _mod_spec(sh, tm, t),
                  _const_spec((1, D_MODEL)), _const_spec((D_MODEL, _N_PROJ)), _const_spec((_KV6, D_MODEL))],
        out_specs=[_row_spec(tm, D_MODEL),
                   pl.BlockSpec((None, _KV6, tm), lambda i: (i // per_b, 0, i % per_b)),
                   _row_spec(tm, 2 * KV_WIDTH)] + tail_specs,
        out_shape=[jax.ShapeDtypeStruct((n, D_MODEL), BF16),
                   jax.ShapeDtypeStruct((b, _KV6, t), F32),
                   jax.ShapeDtypeStruct((n, 2 * KV_WIDTH), F32)] + tail_shapes,
        compiler_params=_cparams(("parallel",)),
        name="proj",
    )(x, sc, sh, gain, w, wt)


def _proj_sample(x, sc, sh, gain, w, tm, rows_per_batch):
    n = x.shape[0]
    tail_specs, tail_shapes = _proj_tail_out(n, tm)
    return pl.pallas_call(
        _proj_sample_kernel,
        grid=(n // tm,),
        in_specs=[_row_spec(tm, D_MODEL), _mod_spec(sc, tm, rows_per_batch), _mod_spec(sh, tm, rows_per_batch),
                  _const_spec((1, D_MODEL)), _const_spec((D_MODEL, _N_PROJ))],
        out_specs=[_row_spec(tm, D_MODEL), _row_spec(tm, _KV6)] + tail_specs,
        out_shape=[jax.ShapeDtypeStruct((n, D_MODEL), BF16),
                   jax.ShapeDtypeStruct((n, _KV6), F32)] + tail_shapes,
        compiler_params=_cparams(("parallel",)),
        name="proj_sample",
    )(x, sc, sh, gain, w)


def _proj_weight(w_in_l):
    q = w_in_l[:, :1024] * ATTN_SCALE
    kv = w_in_l[:, 1024:2560]
    g = w_in_l[:, 2560:2608]
    glu = w_in_l[:, 2608:3632]
    mg = w_in_l[:, 3632:5680]
    pad = jnp.zeros((D_MODEL, _N_PROJ - _OFF_G - 48), F32)
    return jnp.concatenate([q, kv, glu, mg, g, pad], axis=1).astype(BF16), kv.T.astype(BF16)


_POS_PAIRS = L_CMP // 2


def _gelu_tanh(x):
    return x * (0.5 * (1.0 + jnp.tanh(GELU_C * (x + 0.044715 * (x * x * x)))))


def _compress_kernel(xe_ref, xo_ref, wc_ref, b1_ref, w2_ref, kc_ref, vc_ref, acc_ref):
    pp = pl.program_id(1)

    @pl.when(pp == 0)
    def _():
        acc_ref[...] = jnp.zeros_like(acc_ref)

    for kv in range(2):
        for gh in range(2):
            c0 = kv * KV_WIDTH + gh * LANE
            chunk = jnp.concatenate([xe_ref[:, c0:c0 + LANE], xo_ref[:, c0:c0 + LANE]], axis=1)
            a0 = (kv * 2 + gh) * 256
            acc_ref[:, a0:a0 + 256] += _dot(chunk.astype(BF16), wc_ref[kv])

    @pl.when(pp == _POS_PAIRS - 1)
    def _():
        for kv, o_ref in ((0, kc_ref), (1, vc_ref)):
            halves = []
            for gh in range(2):
                a0 = (kv * 2 + gh) * 256
                h = _gelu_tanh(acc_ref[:, a0:a0 + 256] + b1_ref[kv])
                halves.append(_dot(h.astype(BF16), w2_ref[kv]))
            o_ref[...] = jnp.concatenate(halves, axis=1)


def _compress_weights(w1, b1, w2):
    eye2 = jnp.eye(2, dtype=F32)
    w1r = w1.reshape(2, _POS_PAIRS, 2, HEAD_DIM, CMP_HIDDEN)
    wc = jnp.einsum("kpidh,gG->kpigdGh", w1r, eye2).reshape(2, _POS_PAIRS, 256, 256).astype(BF16)
    b1t = jnp.concatenate([b1, b1], axis=-1).reshape(2, 1, 256)
    w2e = jnp.einsum("khd,gG->kghGd", w2, eye2).reshape(2, 256, LANE).astype(BF16)
    return wc, b1t, w2e


def _compress(x2, wc, b1t, w2e, r):
    nb = x2.shape[0]
    blk = lambda off: pl.BlockSpec((r, 2 * KV_WIDTH), lambda i, pp: (i, 2 * pp + off))
    out = pl.BlockSpec((r, KV_WIDTH), lambda i, pp: (i, 0))
    return pl.pallas_call(
        _compress_kernel,
        grid=(nb // r, _POS_PAIRS),
        in_specs=[blk(0), blk(1),
                  pl.BlockSpec((2, None, 256, 256), lambda i, pp: (0, pp, 0, 0)),
                  _const_spec((2, 1, 256)), _const_spec((2, 256, LANE))],
        out_specs=[out, out],
        out_shape=[jax.ShapeDtypeStruct((nb, KV_WIDTH), F32)] * 2,
        scratch_shapes=[pltpu.VMEM((r, 1024), F32)],
        compiler_params=_cparams(("parallel", "arbitrary")),
        name="compress",
    )(x2, x2, wc, b1t, w2e)


_PAGE = 128
_PP = 32
_BPP = _PAGE // L_CMP
_PITCH = L_CMP + 4


def _compress_pool_kernel(x_ref, wc_ref, b1_ref, w2_ref, o_ref, xs_ref):
    def relayout(j, carry):
        r0 = pl.multiple_of(j * (_BPP * _PITCH), 8)
        for kv in range(2):
            x = x_ref[j, kv].T
            for gh in range(2):
                for b in range(_BPP):
                    xs_ref[kv * 2 + gh, pl.ds(r0 + b * _PITCH, L_CMP), :] = (
                        x[b * L_CMP:(b + 1) * L_CMP, gh * LANE:(gh + 1) * LANE])
        return carry

    lax.fori_loop(0, _PP, relayout, 0)
    nblk = _PP * _BPP
    for kv in range(2):
        for gh in range(2):
            acc = jnp.zeros((nblk, 256), F32)
            for pp in range(_POS_PAIRS):
                a0 = xs_ref[kv * 2 + gh, pl.ds(2 * pp, nblk, stride=_PITCH), :]
                a1 = xs_ref[kv * 2 + gh, pl.ds(2 * pp + 1, nblk, stride=_PITCH), :]
                chunk = jnp.concatenate([a0, a1], axis=1).astype(BF16)
                acc = acc + _dot(chunk, wc_ref[kv, pp])
            h = _gelu_tanh(acc + b1_ref[kv])
            c0 = kv * KV_WIDTH + gh * LANE
            o_ref[:, c0:c0 + LANE] = _dot(h.astype(BF16), w2_ref[kv])


def _compress_pool(cache_t, layer, wc, b1t, w2e):
    n_pool = cache_t.shape[1]
    return pl.pallas_call(
        _compress_pool_kernel,
        grid=(n_pool // _PP,),
        in_specs=[pl.BlockSpec((None, _PP, 2, KV_WIDTH, _PAGE), lambda i: (layer, i, 0, 0, 0)),
                  _const_spec((2, _POS_PAIRS, 256, 256)), _const_spec((2, 1, 256)),
                  _const_spec((2, 256, LANE))],
        out_specs=pl.BlockSpec((_PP * _BPP, 2 * KV_WIDTH), lambda i: (i, 0)),
        out_shape=jax.ShapeDtypeStruct((n_pool * _BPP, 2 * KV_WIDTH), F32),
        scratch_shapes=[pltpu.VMEM((4, _PP * _BPP * _PITCH, LANE), F32)],
        compiler_params=_cparams(("parallel",)),
        name="compress_pool",
    )(cache_t, wc, b1t, w2e)


def _topk_member(score, n_rows):
    sb = lax.broadcasted_iota(jnp.int32, score.shape, 0)
    cnt = jnp.zeros(score.shape, jnp.int32)
    for jj in range(n_rows):
        row = score[jj:jj + 1, :]
        beats = (row > score) | ((row == score) & (jj < sb))
        cnt = cnt + jnp.where(beats, 1, 0)
    return cnt < N_SEL


_TQ = 256


def _select_kernel(q_ref, kcp_ref, qa_ref):
    t0 = pl.program_id(2) * _TQ
    qpos = t0 + lax.broadcasted_iota(jnp.int32, (1, _TQ), 1)
    r = lax.broadcasted_iota(jnp.int32, (2 * 32, 1), 0)
    blk = jnp.where(r < 32, 2 * r, 2 * (r - 32) + 1)
    mask_c = ((blk + 1) * L_CMP - 1) <= qpos
    q = q_ref[...]
    kcp = kcp_ref[...]
    p_slc = jnp.zeros((32, _TQ), F32)
    for j in range(GROUP):
        s = _dot_nt(kcp, q[:, j * HEAD_DIM:(j + 1) * HEAD_DIM])
        s = jnp.where(mask_c, s, -jnp.inf)
        m = jnp.max(s, axis=0, keepdims=True)
        m = jnp.where(m == -jnp.inf, 0.0, m)
        p = jnp.exp(s - m)
        d = jnp.sum(p, axis=0, keepdims=True)
        p = p / jnp.where(d > 0, d, 1.0)
        p_slc = p_slc + (p[:32] + p[32:])
    sb = lax.broadcasted_iota(jnp.int32, (32, 1), 0)
    cur = qpos // L_SLC
    forced = (sb == 0) | (sb == cur) | (sb == cur - 1)
    score = p_slc + jnp.where(forced, FORCE_BONUS, 0.0)
    score = jnp.where(sb * L_SLC <= qpos, score, NEG_SCORE)
    sel = _topk_member(score, 32)
    bias_t = jnp.where(sel, 0.0, MASK_BIAS)
    bias = jnp.concatenate([bias_t, jnp.zeros((LANE - 32, _TQ), F32)], axis=0).T
    for j in range(GROUP):
        qh = q[:, j * HEAD_DIM:(j + 1) * HEAD_DIM].astype(F32)
        qa_ref[j] = jnp.concatenate([qh, bias[:, :LANE - HEAD_DIM]], axis=1).astype(BF16)


def _select(q, kcp, b, t):
    return pl.pallas_call(
        _select_kernel,
        grid=(b, N_KV, t // _TQ),
        in_specs=[pl.BlockSpec((None, _TQ, KV_WIDTH), lambda bi, g, i: (bi, i, g)),
                  pl.BlockSpec((None, None, 64, HEAD_DIM), lambda bi, g, i: (bi, g, 0, 0))],
        out_specs=pl.BlockSpec((None, None, GROUP, _TQ, LANE), lambda bi, g, i: (bi, g, 0, i, 0)),
        out_shape=jax.ShapeDtypeStruct((b, N_KV, GROUP, t, LANE), BF16),
        compiler_params=_cparams(("parallel", "parallel", "parallel")),
        name="select",
    )(q, kcp)


_TK = 256
_ROWS = GROUP * _TQ
_ONES_ROW = HEAD_DIM


def _two_pass_attention(q, qi, row_t, k_scr, v_scr, n_tiles, far_window_mask, s_all, m_ref, acc_ref):
    kiota = lax.broadcasted_iota(jnp.int32, (1, _TK), 1)

    def scores(i, mask):
        kt = qi - i
        s = _dot(q, k_scr[kt])
        if mask is not None:
            s = jnp.where(mask(row_t - (kt * _TK + kiota)), s, -jnp.inf)
        s_all[i] = s
        return jnp.maximum(s[:, :LANE], s[:, LANE:])

    m_ref[...] = scores(0, lambda dist: dist >= 0)

    def pass1(i, carry):
        mask = (lambda dist: dist < WINDOW) if far_window_mask else None
        m_ref[...] = jnp.maximum(m_ref[...], scores(i, mask))
        return carry

    lax.fori_loop(1, n_tiles, pass1, 0)
    m_ref[...] = jnp.broadcast_to(jnp.max(m_ref[...], axis=-1, keepdims=True), m_ref.shape)
    acc_ref[...] = jnp.zeros(acc_ref.shape, F32)

    def pass2(i, carry):
        s = s_all[i]
        m = m_ref[...]
        p = jnp.exp(jnp.concatenate([s[:, :LANE] - m, s[:, LANE:] - m], axis=1)).astype(BF16)
        acc_ref[...] += _dot_nt(p, v_scr[qi - i])
        return carry

    lax.fori_loop(0, n_tiles, pass2, 0)
    acc = acc_ref[...]
    return acc[:, :HEAD_DIM] / acc[:, _ONES_ROW:_ONES_ROW + 1]


def _attn_kernel(qa_ref, ks_ref, vs_ref, kw_ref, vw_ref, kc_ref, vc_ref, g_ref, o_ref,
                 ks_scr, vs_scr, kw_scr, vw_scr, s_all, m_ref, acc_ref):
    qi = pl.program_id(2)
    n_kt = ks_scr.shape[0]

    @pl.when(qi == 0)
    def _():
        blk_row = lax.broadcasted_iota(jnp.int32, (32, _TK), 0)
        ones_row = jnp.where(lax.broadcasted_iota(jnp.int32, (16, _TK), 0) == 0, 1.0, 0.0).astype(BF16)
        for kt in range(n_kt):
            cols = slice(kt * _TK, (kt + 1) * _TK)
            lane_blk = (kt * _TK + lax.broadcasted_iota(jnp.int32, (32, _TK), 1)) // L_SLC
            ks_scr[kt, 0:HEAD_DIM, :] = ks_ref[:, cols].astype(BF16)
            ks_scr[kt, HEAD_DIM:HEAD_DIM + 32, :] = jnp.where(lane_blk == blk_row, 1.0, 0.0).astype(BF16)
            ks_scr[kt, HEAD_DIM + 32:, :] = jnp.zeros((LANE - HEAD_DIM - 32, _TK), BF16)
            kw_scr[kt, 0:HEAD_DIM, :] = kw_ref[:, cols].astype(BF16)
            kw_scr[kt, HEAD_DIM:, :] = jnp.zeros((LANE - HEAD_DIM, _TK), BF16)
            for v_ref, v_scr in ((vs_ref, vs_scr), (vw_ref, vw_scr)):
                v_scr[kt, 0:HEAD_DIM, :] = v_ref[:, cols].astype(BF16)
                v_scr[kt, HEAD_DIM:HEAD_DIM + 16, :] = ones_row
                v_scr[kt, HEAD_DIM + 16:, :] = jnp.zeros((LANE - HEAD_DIM - 16, _TK), BF16)

    t0 = qi * _TQ
    q = qa_ref[...].reshape(_ROWS, LANE)
    row_t = t0 + (lax.broadcasted_iota(jnp.int32, (_ROWS, 1), 0) & (_TQ - 1))

    s = _dot(q, kc_ref[...])
    cb_end = (lax.broadcasted_iota(jnp.int32, (1, 64), 1) + 1) * L_CMP - 1
    s = jnp.where(cb_end <= row_t, s, -jnp.inf)
    m = jnp.max(s, axis=-1, keepdims=True)
    m = jnp.where(m == -jnp.inf, 0.0, m)
    oc = _dot_nt(jnp.exp(s - m).astype(BF16), vc_ref[...])
    den = oc[:, _ONES_ROW:_ONES_ROW + 1]
    o_cmp = oc[:, :HEAD_DIM] / jnp.where(den > 0, den, 1.0)

    o_slc = _two_pass_attention(q, qi, row_t, ks_scr, vs_scr, qi + 1, False, s_all, m_ref, acc_ref)
    o_win = _two_pass_attention(q, qi, row_t, kw_scr, vw_scr, jnp.minimum(qi, WINDOW // _TK) + 1,
                                True, s_all, m_ref, acc_ref)

    g = g_ref[...]
    heads = []
    for j in range(GROUP):
        rs = slice(j * _TQ, (j + 1) * _TQ)
        c = N_BRANCH * j
        heads.append(g[:, c:c + 1] * o_cmp[rs] + g[:, c + 1:c + 2] * o_slc[rs]
                     + g[:, c + 2:c + 3] * o_win[rs])
    o_ref[...] = jnp.concatenate(heads, axis=1)


def _attn(qa, kvt, kc, vc, gates, b, t):
    head_t = lambda c: pl.BlockSpec((None, None, HEAD_DIM, t), lambda bi, g, i: (bi, N_KV * c + g, 0, 0))
    summ = pl.BlockSpec((None, None, LANE, 64), lambda bi, g, i: (bi, g, 0, 0))
    n_kt = t // _TK
    tiles = pltpu.VMEM((n_kt, LANE, _TK), BF16)
    return pl.pallas_call(
        _attn_kernel,
        grid=(b, N_KV, t // _TQ),
        in_specs=[pl.BlockSpec((None, None, GROUP, _TQ, LANE), lambda bi, g, i: (bi, g, 0, i, 0)),
                  head_t(2), head_t(3), head_t(4), head_t(5), summ, summ,
                  pl.BlockSpec((None, None, _TQ, LANE), lambda bi, g, i: (bi, g, i, 0))],
        out_specs=pl.BlockSpec((None, _TQ, KV_WIDTH), lambda bi, g, i: (bi, i, g)),
        out_shape=jax.ShapeDtypeStruct((b, t, D_MODEL), F32),
        scratch_shapes=[tiles, tiles, tiles, tiles,
                        pltpu.VMEM((n_kt, _ROWS, _TK), F32),
                        pltpu.VMEM((_ROWS, LANE), F32), pltpu.VMEM((_ROWS, LANE), F32)],
        compiler_params=_cparams(("arbitrary", "arbitrary", "arbitrary")),
        name="attn",
    )(qa, kvt, kvt, kvt, kvt, kc, vc, gates)


_N_PAGES = 16
_PAST = _N_PAGES * _PAGE
_SROWS = GROUP * N_KV * 4
_NEWPAD = 128
_NSB_PAD = 40
_NEWROWS = 8


def _diag_heads(o, row_g):
    col_g = lax.broadcasted_iota(jnp.int32, (1, KV_WIDTH), 1) // HEAD_DIM
    om = jnp.where(col_g == row_g, o, 0.0)
    return (om[:, 0:64] + om[:, 64:128]) + (om[:, 128:192] + om[:, 192:256])


def _softmax_rows(s):
    m = jnp.max(s, axis=-1, keepdims=True)
    m = jnp.where(m == -jnp.inf, 0.0, m)
    p = jnp.exp(s - m)
    d = jnp.sum(p, axis=-1, keepdims=True)
    return p / jnp.where(d > 0, d, 1.0)


def _softmax_two(s_old, s_new):
    m = jnp.maximum(jnp.max(s_old, axis=-1, keepdims=True), jnp.max(s_new, axis=-1, keepdims=True))
    m = jnp.where(m == -jnp.inf, 0.0, m)
    p_old = jnp.exp(s_old - m)
    p_new = jnp.exp(s_new - m)
    d = jnp.sum(p_old, axis=-1, keepdims=True) + jnp.sum(p_new, axis=-1, keepdims=True)
    inv = 1.0 / jnp.where(d > 0, d, 1.0)
    return (p_old * inv).astype(BF16), (p_new * inv).astype(BF16)


def _sample_attn_kernel(dt, pt_ref, qbd_ref, *refs):
    del pt_ref
    page_refs = refs[:_N_PAGES]
    sum_refs = refs[_N_PAGES:2 * _N_PAGES]
    kvn_ref, wold_ref, wnewt_ref, g_ref, oh_ref, o_ref, wout_ref, sum_scr, kcp_scr = refs[2 * _N_PAGES:]

    qbd = qbd_ref[...]
    rows = lax.broadcasted_iota(jnp.int32, (_SROWS, 1), 0)
    row_t = rows % dt
    row_g = (rows // dt) % N_KV

    per_page = _PAGE // L_CMP
    for j in range(_N_PAGES):
        sum_scr[per_page * j:per_page * (j + 1), :] = sum_refs[j][...]
        for i in range(per_page):
            dst = (i % 2) * 32 + (per_page * j + i) // 2
            kcp_scr[dst:dst + 1, :] = sum_refs[j][i:i + 1, :KV_WIDTH]
    kvc = sum_scr[...]
    p = _softmax_rows(_dot_nt(qbd, kvc[:, :KV_WIDTH].astype(BF16)))
    o_cmp = _diag_heads(_dot(p.astype(BF16), kvc[:, KV_WIDTH:].astype(BF16)), row_g)

    s_t = _dot_nt(kcp_scr[...].astype(BF16), qbd)
    m = jnp.max(s_t, axis=0, keepdims=True)
    pt = jnp.exp(s_t - m)
    pt = pt / jnp.sum(pt, axis=0, keepdims=True)
    pp = pt[:32] + pt[32:]
    nq = N_KV * dt
    p_slc = (pp[:, 0:nq] + pp[:, nq:2 * nq]) + (pp[:, 2 * nq:3 * nq] + pp[:, 3 * nq:4 * nq])
    p_slc = jnp.concatenate([p_slc, jnp.zeros((_NSB_PAD - 32, nq), F32)], axis=0)
    sb = lax.broadcasted_iota(jnp.int32, (_NSB_PAD, 1), 0)
    qpos = _PAST + lax.broadcasted_iota(jnp.int32, (1, nq), 1) % dt
    cur = qpos // L_SLC
    forced = (sb == 0) | (sb == cur) | (sb == cur - 1)
    score = p_slc + jnp.where(forced, FORCE_BONUS, 0.0)
    score = jnp.where(sb * L_SLC <= qpos, score, NEG_SCORE)
    score = jnp.where(sb < _PAST // L_SLC + 1, score, -jnp.inf)
    sel_t = _topk_member(score, _PAST // L_SLC + 1)
    bias_t = jnp.where(sel_t, 0.0, MASK_BIAS)
    bias_t = jnp.concatenate([bias_t, jnp.zeros((LANE - _NSB_PAD, nq), F32)], axis=0)
    bias_t = jnp.concatenate([bias_t, jnp.zeros((LANE, LANE - nq), F32)], axis=1)
    bias = bias_t.T[:nq]
    bias = jnp.concatenate([bias] * GROUP, axis=0).astype(BF16)

    knew = kvn_ref[...]
    zpad = jnp.zeros((_NEWPAD - _NEWROWS, KV_WIDTH), F32)
    new_tm = lambda c: jnp.concatenate([knew[:, c * KV_WIDTH:(c + 1) * KV_WIDTH], zpad], axis=0).astype(BF16)
    nidx = lax.broadcasted_iota(jnp.int32, (1, _NEWPAD), 1)
    kt_all = jnp.concatenate([r[0] for r in page_refs], axis=1).astype(BF16)
    vt_all = jnp.concatenate([r[1] for r in page_refs], axis=1).astype(BF16)
    oh = oh_ref[...]
    s_old = _dot(qbd, kt_all) + _dot(bias, oh[:, :_PAST])
    s_new = _dot_nt(qbd, new_tm(2)) + _dot(bias, oh[:, _PAST:])
    s_new = jnp.where(nidx <= row_t, s_new, -jnp.inf)
    p_old, p_new = _softmax_two(s_old, s_new)
    o_slc = _diag_heads(_dot_nt(p_old, vt_all) + _dot(p_new, new_tm(3)), row_g)

    wb = wold_ref.shape[2]
    s_old = _dot(qbd, wold_ref[0].astype(BF16))
    widx = lax.broadcasted_iota(jnp.int32, (1, wb), 1)
    dist = row_t + (wb - widx)
    s_old = jnp.where((dist >= 0) & (dist < WINDOW), s_old, -jnp.inf)
    s_new = _dot_nt(qbd, new_tm(4))
    dist = row_t - nidx
    s_new = jnp.where((dist >= 0) & (dist < WINDOW), s_new, -jnp.inf)
    p_old, p_new = _softmax_two(s_old, s_new)
    o_win = _diag_heads(_dot_nt(p_old, wold_ref[1].astype(BF16)) + _dot(p_new, new_tm(5)), row_g)

    g = g_ref[...]
    o_ref[...] = g[:, 0:1] * o_cmp + g[:, 1:2] * o_slc + g[:, 2:3] * o_win

    lane = lax.broadcasted_iota(jnp.int32, (1, LANE), 1)
    for c in range(2):
        shifted = pltpu.roll(wold_ref[c], wb - dt, axis=1)
        wout_ref[c, :, 0:wb - LANE] = shifted[:, 0:wb - LANE]
        wout_ref[c, :, wb - LANE:wb] = jnp.where(lane >= LANE - dt, wnewt_ref[c], shifted[:, wb - LANE:wb])


def _sample_attn(page_table, qbd, cache_t, layer, kvc_pool, kvn, win_t, wnewt, grow, onehot, dt):
    nb = qbd.shape[0]
    wb = win_t.shape[-1]
    kvn = jnp.pad(kvn, ((0, 0), (0, _NEWROWS - dt), (0, 0)))
    page = lambda j: pl.BlockSpec((None, None, 2, KV_WIDTH, _PAGE),
                                  lambda b, pt: (layer, pt[b * _N_PAGES + j], 1, 0, 0))
    summ = lambda j: pl.BlockSpec((None, _BPP, 2 * KV_WIDTH),
                                  lambda b, pt: (pt[b * _N_PAGES + j], 0, 0))
    per_seq = lambda *shape: pl.BlockSpec((None,) + shape, lambda b, pt: (b,) + (0,) * len(shape))
    grid_spec = pltpu.PrefetchScalarGridSpec(
        num_scalar_prefetch=1,
        grid=(nb,),
        in_specs=[per_seq(_SROWS, KV_WIDTH)]
        + [page(j) for j in range(_N_PAGES)] + [summ(j) for j in range(_N_PAGES)]
        + [per_seq(_NEWROWS, _KV6),
           pl.BlockSpec((None, None, 2, KV_WIDTH, wb), lambda b, pt: (layer, b, 0, 0, 0)),
           per_seq(2, KV_WIDTH, LANE), per_seq(_SROWS, LANE),
           pl.BlockSpec((LANE, _PAST + _NEWPAD), lambda b, pt: (0, 0))],
        out_specs=[per_seq(_SROWS, HEAD_DIM), per_seq(2, KV_WIDTH, wb)],
        scratch_shapes=[pltpu.VMEM((_BPP * _N_PAGES, 2 * KV_WIDTH), F32),
                        pltpu.VMEM((_BPP * _N_PAGES, KV_WIDTH), F32)],
    )
    return pl.pallas_call(
        functools.partial(_sample_attn_kernel, dt),
        grid_spec=grid_spec,
        out_shape=[jax.ShapeDtypeStruct((nb, _SROWS, HEAD_DIM), F32),
                   jax.ShapeDtypeStruct((nb, 2, KV_WIDTH, wb), F32)],
        compiler_params=_cparams(("parallel",)),
        name="sample_attn",
    )(page_table.reshape(-1), qbd, *([cache_t] * _N_PAGES), *([kvc_pool] * _N_PAGES),
      kvn, win_t, wnewt, grow, onehot)


_TT = 256
_HALO = 32
_RC = 64


def _dwconv_kernel(cur_ref, prev_ref, w_ref, b_ref, y_ref, h_ref):
    prev = prev_ref[...]
    h_ref[0:_HALO, :] = jnp.where(pl.program_id(1) == 0, 0.0, prev)
    h_ref[_HALO:, :] = cur_ref[...]
    off = _HALO - (CONV_W - 1)
    for r0 in range(0, _TT, _RC):
        acc = jnp.broadcast_to(b_ref[...], (_RC, C_CONV))
        for k in range(CONV_W):
            acc = acc + w_ref[k:k + 1, :] * h_ref[r0 + off + k:r0 + off + k + _RC, :]
        y_ref[r0:r0 + _RC, :] = acc


def _dwconv(conv_in, w_pad, b_dw):
    b, t, _ = conv_in.shape
    return pl.pallas_call(
        _dwconv_kernel,
        grid=(b, t // _TT),
        in_specs=[pl.BlockSpec((None, _TT, C_CONV), lambda bi, i: (bi, i, 0)),
                  pl.BlockSpec((None, _HALO, C_CONV),
                               lambda bi, i: (bi, jnp.maximum(i * (_TT // _HALO) - 1, 0), 0)),
                  _const_spec((_HALO, C_CONV)), _const_spec((1, C_CONV))],
        out_specs=pl.BlockSpec((None, _TT, C_CONV), lambda bi, i: (bi, i, 0)),
        out_shape=jax.ShapeDtypeStruct((b, t, C_CONV), F32),
        scratch_shapes=[pltpu.VMEM((_TT + _HALO, C_CONV), F32)],
        compiler_params=_cparams(("parallel", "parallel")),
        name="dwconv",
    )(conv_in, conv_in, w_pad, b_dw)


_SB = 8


def _dwconv_sample_kernel(h_ref, w_ref, b_ref, y_ref):
    dt = y_ref.shape[0]
    w = w_ref[0:CONV_W, :]
    for t in range(dt):
        y_ref[t:t + 1, :] = jnp.sum(h_ref[t:t + CONV_W, :] * w, axis=0, keepdims=True) + b_ref[...]


def _dwconv_sample(h2, w_pad, b_dw, b, c):
    rows = h2.shape[0]
    dt = rows - (CONV_W - 1)
    w_t = jnp.tile(w_pad, (1, _SB))
    b_t = jnp.tile(b_dw, (1, _SB))
    return pl.pallas_call(
        _dwconv_sample_kernel,
        grid=(b // _SB,),
        in_specs=[pl.BlockSpec((rows, _SB * c), lambda i: (0, i)),
                  _const_spec((_HALO, _SB * c)), _const_spec((1, _SB * c))],
        out_specs=pl.BlockSpec((dt, _SB * c), lambda i: (0, i)),
        out_shape=jax.ShapeDtypeStruct((dt, b * c), F32),
        compiler_params=_cparams(("parallel",)),
        name="dwconv_sample",
    )(h2, w_t, b_t)


def _mix_out_kernel(y_ref, lng_ref, lnb_ref, wpw_ref, on_ref, ga_ref, gb_ref, wo_ref,
                    x_ref, gt_ref, gain_ref, o_ref):
    y = y_ref[...]
    yc = y - jnp.mean(y, axis=-1, keepdims=True)
    var = jnp.mean(yc * yc, axis=-1, keepdims=True)
    yn = yc * lax.rsqrt(var + EPS) * lng_ref[...] + lnb_ref[...]
    o_conv = _dot((yn * _sigmoid(yn)).astype(BF16), wpw_ref[...])
    merged = ga_ref[...] * on_ref[...] + gb_ref[...] * o_conv
    yo = _dot(merged.astype(BF16), wo_ref[...])
    o_ref[...] = x_ref[...] + gt_ref[...] * _rms(yo, gain_ref[...])


def _mix_out(ydw, lng, lnb, wpw, o_nsa, ga, gb, wo, x, gt, gain, tm, rows_per_batch):
    n = x.shape[0]
    row = lambda c: _row_spec(tm, c)
    return pl.pallas_call(
        _mix_out_kernel,
        grid=(n // tm,),
        in_specs=[row(C_CONV), _const_spec((1, C_CONV)), _const_spec((1, C_CONV)),
                  _const_spec((C_CONV, D_MODEL)), row(D_MODEL), row(D_MODEL), row(D_MODEL),
                  _const_spec((D_MODEL, D_MODEL)), row(D_MODEL), _mod_spec(gt, tm, rows_per_batch),
                  _const_spec((1, D_MODEL))],
        out_specs=row(D_MODEL),
        out_shape=jax.ShapeDtypeStruct((n, D_MODEL), F32),
        compiler_params=_cparams(("parallel",)),
        name="mix_out",
    )(ydw, lng, lnb, wpw, o_nsa, ga, gb, wo, x, gt, gain)


_FC = D_FFN // 2


def _ffn_kernel(x_ref, sc_ref, sh_ref, gt_ref, g2_ref, g3_ref, wi_ref, wo_ref, o_ref):
    x = x_ref[...]
    h = (_rms(x, g2_ref[...]) * (1.0 + sc_ref[...]) + sh_ref[...]).astype(BF16)
    y = jnp.zeros(x.shape, F32)
    for c in range(0, D_FFN, _FC):
        a = _dot(h, wi_ref[:, c:c + _FC])
        b = _dot(h, wi_ref[:, D_FFN + c:D_FFN + c + _FC])
        u = (a * _sigmoid(a) * b).astype(BF16)
        y = y + _dot(u, wo_ref[c:c + _FC, :])
    o_ref[...] = x + gt_ref[...] * _rms(y, g3_ref[...])


def _ffn(x, sc, sh, gt, g2, g3, wi, wo, tm, rows_per_batch):
    n = x.shape[0]
    row = _row_spec(tm, D_MODEL)
    mod = lambda a: _mod_spec(a, tm, rows_per_batch)
    return pl.pallas_call(
        _ffn_kernel,
        grid=(n // tm,),
        in_specs=[row, mod(sc), mod(sh), mod(gt), _const_spec((1, D_MODEL)), _const_spec((1, D_MODEL)),
                  _const_spec((D_MODEL, 2 * D_FFN)), _const_spec((D_FFN, D_MODEL))],
        out_specs=row,
        out_shape=jax.ShapeDtypeStruct((n, D_MODEL), F32),
        compiler_params=_cparams(("parallel",)),
        name="ffn",
    )(x, sc, sh, gt, g2, g3, wi, wo)


def _summaries_t(s, b, n_cb, ones_row):
    st = s.reshape(b, n_cb, N_KV, HEAD_DIM).transpose(0, 2, 3, 1)
    extra = jnp.zeros((b, N_KV, LANE - HEAD_DIM, n_cb), F32)
    if ones_row:
        extra = extra.at[:, :, 0, :].set(1.0)
    return jnp.concatenate([st, extra], axis=2).astype(BF16)


def _prompt_mixer(q, kvt, gates, kc, vc, b, t):
    n_cb = t // L_CMP
    kch = kc.reshape(b, n_cb, N_KV, HEAD_DIM).transpose(0, 2, 1, 3).astype(BF16)
    kcp = jnp.concatenate([kch[:, :, 0::2], kch[:, :, 1::2]], axis=2)
    g4 = gates[:, :N_HEADS * N_BRANCH].reshape(b, t, N_KV, GROUP * N_BRANCH).transpose(0, 2, 1, 3)
    g4 = jnp.concatenate([g4, jnp.zeros((b, N_KV, t, LANE - GROUP * N_BRANCH), F32)], axis=-1)
    qa = _select(q.reshape(b, t, D_MODEL), kcp, b, t)
    o = _attn(qa, kvt.reshape(b, 6 * N_KV, HEAD_DIM, t), _summaries_t(kc, b, n_cb, False),
              _summaries_t(vc, b, n_cb, True), g4, b, t)
    return o.reshape(b * t, D_MODEL)


def _sample_mixer(q, kv, gates, kvc_pool, cache_t, layer, page_table, win_t, b, dt):
    qr = q.reshape(b, dt, N_KV, GROUP, HEAD_DIM).transpose(0, 3, 2, 1, 4)
    eye = jnp.eye(N_KV, dtype=BF16)
    qbd = (qr[:, :, :, :, None, :] * eye[None, None, :, None, :, None]).reshape(b, _SROWS, KV_WIDTH)
    gr = gates[:, :N_HEADS * N_BRANCH].reshape(b, dt, N_KV, GROUP, N_BRANCH).transpose(0, 3, 2, 1, 4)
    gr = gr.reshape(b, _SROWS, N_BRANCH)
    gr = jnp.concatenate([gr, jnp.zeros((b, _SROWS, LANE - N_BRANCH), F32)], axis=-1)
    nkeys = _PAST + _NEWPAD
    onehot = (jnp.arange(nkeys)[None, :] // L_SLC == jnp.arange(LANE)[:, None]).astype(BF16)
    kv3 = kv.reshape(b, dt, _KV6)
    wnewt = kv3[:, :, 4 * KV_WIDTH:].transpose(0, 2, 1).reshape(b, 2, KV_WIDTH, dt)
    wnewt = jnp.pad(wnewt, ((0, 0), (0, 0), (0, 0), (LANE - dt, 0)))
    o, wout = _sample_attn(page_table, qbd, cache_t, layer, kvc_pool, kv3, win_t, wnewt, gr, onehot, dt)
    o = o.reshape(b, GROUP, N_KV, dt, HEAD_DIM).transpose(0, 3, 2, 1, 4)
    return o.reshape(b * dt, D_MODEL), wout


def _feature_major(a):
    lead = a.shape[:-4]
    rows, p = a.shape[-4], a.shape[-3]
    nd = len(lead)
    perm = tuple(range(nd)) + (nd + 1, nd + 2, nd + 3, nd)
    return a.transpose(perm).reshape(lead + (p, KV_WIDTH, rows))


def _token_major(a_t, p):
    b, _, rows = a_t.shape
    return a_t.reshape(b, p, N_KV, HEAD_DIM, rows).transpose(0, 4, 1, 2, 3)


def kernel(x_prompt, x_sample, cache_kv, state_win, state_conv, page_table, c_prompt, c_sample,
           w_ada, b_ada, norm_gain, w_in, w_cmp1, b_cmp1, w_cmp2, w_dw, b_dw, ln_conv_g,
           ln_conv_b, w_pw2, w_out, w_ffn_in, w_ffn_out):
    bp, t, _ = x_prompt.shape
    bs, dt, _ = x_sample.shape
    n_pool = cache_kv.shape[1]
    tm_p, tm_s = 256, 256

    ada = _ada(jnp.concatenate([c_prompt, c_sample], axis=0), w_ada, b_ada)
    cache_t = _feature_major(cache_kv)
    win_t = _feature_major(state_win)
    conv_t = state_conv.transpose(0, 2, 1, 3)

    xp = x_prompt.reshape(bp * t, D_MODEL)
    xs = x_sample.reshape(bs * dt, D_MODEL)
    kv_p, kv_s, win_p, win_s, conv_p, conv_s = [], [], [], [], [], []
    for l in range(DEPTH):
        mods_p = [m.reshape(bp, 1, D_MODEL) for m in jnp.split(ada[l, :bp], 6, axis=-1)]
        mods_s = [jnp.repeat(m, dt, axis=0) for m in jnp.split(ada[l, bp:], 6, axis=-1)]
        gain = norm_gain[l].reshape(4, 1, D_MODEL)
        wproj, wkv_t = _proj_weight(w_in[l])
        wc, b1t, w2e = _compress_weights(w_cmp1[l], b_cmp1[l], w_cmp2[l])
        w_pad = jnp.concatenate([w_dw[l], jnp.zeros((_HALO - CONV_W, C_CONV), F32)], axis=0)
        bdw = b_dw[l].reshape(1, C_CONV)
        lng = ln_conv_g[l].reshape(1, C_CONV)
        lnb = ln_conv_b[l].reshape(1, C_CONV)
        wpw = w_pw2[l].astype(BF16)
        wo = w_out[l].astype(BF16)
        wfi = w_ffn_in[l].astype(BF16)
        wfo = w_ffn_out[l].astype(BF16)

        sh_m, sc_m, gt_m, sh_f, sc_f, gt_f = mods_p
        q, kvt, cmp_tm, conv_in, ga, gb, gates = _proj_prompt(xp, sc_m, sh_m, gain[0], wproj, wkv_t,
                                                              tm_p, bp, t)
        n_blk = bp * t // L_CMP
        kc, vc = _compress(cmp_tm.reshape(n_blk, L_CMP * 2 * KV_WIDTH), wc, b1t, w2e, n_blk)
        o_nsa = _prompt_mixer(q, kvt, gates, kc, vc, bp, t)
        conv_in3 = conv_in.reshape(bp, t, C_CONV)
        ydw = _dwconv(conv_in3, w_pad, bdw).reshape(bp * t, C_CONV)
        xp = _mix_out(ydw, lng, lnb, wpw, o_nsa, ga, gb, wo, xp, gt_m, gain[1], tm_p, t)
        xp = _ffn(xp, sc_f, sh_f, gt_f, gain[2], gain[3], wfi, wfo, tm_p, t)
        kv_p.append(_token_major(kvt[:, :4 * KV_WIDTH], 4))
        win_p.append(_token_major(kvt[:, 4 * KV_WIDTH:, t - min(WINDOW, t):], 2))
        conv_p.append(conv_in3[:, t - (CONV_W - 1):])

        sh_m, sc_m, gt_m, sh_f, sc_f, gt_f = mods_s
        q, kv, conv_in, ga, gb, gates = _proj_sample(xs, sc_m, sh_m, gain[0], wproj, tm_s, dt)
        kvc_pool = _compress_pool(cache_t, l, wc, b1t, w2e).reshape(n_pool, _BPP, 2 * KV_WIDTH)
        o_nsa, wout = _sample_mixer(q, kv, gates, kvc_pool, cache_t, l, page_table, win_t, bs, dt)
        h2 = jnp.concatenate([conv_t[l], conv_in.reshape(bs, dt, C_CONV).transpose(1, 0, 2)], axis=0)
        ydw = _dwconv_sample(h2.reshape(CONV_W - 1 + dt, bs * C_CONV), w_pad, bdw, bs, C_CONV)
        ydw = ydw.reshape(dt, bs, C_CONV).transpose(1, 0, 2).reshape(bs * dt, C_CONV)
        xs = _mix_out(ydw, lng, lnb, wpw, o_nsa, ga, gb, wo, xs, gt_m, gain[1], tm_s, dt)
        xs = _ffn(xs, sc_f, sh_f, gt_f, gain[2], gain[3], wfi, wfo, tm_s, dt)
        kv_s.append(kv[:, :4 * KV_WIDTH].reshape(bs, dt, 4, N_KV, HEAD_DIM))
        win_s.append(_token_major(wout.reshape(bs, 2 * KV_WIDTH, -1), 2))
        conv_s.append(h2[dt:].transpose(1, 0, 2))

    return (xp.reshape(bp, t, D_MODEL), xs.reshape(bs, dt, D_MODEL),
            jnp.stack(kv_p), jnp.stack(kv_s), jnp.stack(win_p), jnp.stack(win_s),
            jnp.stack(conv_p), jnp.stack(conv_s))
```

```python
import functools

import jax
import jax.numpy as jnp
import numpy as np
from jax import lax
from jax.experimental import pallas as pl
from jax.experimental.pallas import tpu as pltpu

D_MODEL = 1024
DEPTH = 2
N_HEADS = 16
HEAD_DIM = 64
N_KV = 4
GROUP = 4
KV_WIDTH = N_KV * HEAD_DIM
L_CMP = 32
L_SLC = 64
N_SEL = 16
WINDOW = 512
N_BRANCH = 3
CMP_HIDDEN = 128
C_CONV = 512
CONV_W = 31
D_FFN = 2816
EPS = 1e-6
FORCE_BONUS = 1e4
NEG_SCORE = -1e9
ATTN_SCALE = HEAD_DIM ** -0.5

MASK_BIAS = -(2.0 ** 100)
GELU_C = float(np.float32(np.sqrt(2.0 / np.pi)))

LANE = 128
VMEM_LIMIT = 56 * 1024 * 1024

F32 = jnp.float32
BF16 = jnp.bfloat16

_OFF_Q = 0
_OFF_KV = 1024
_OFF_GLU = 2560
_OFF_MG = 3584
_OFF_G = 5632
_N_PROJ = 5760
_KV6 = 6 * KV_WIDTH


def _cparams(sem):
    return pltpu.CompilerParams(dimension_semantics=sem, vmem_limit_bytes=VMEM_LIMIT)


def _dot(a, b):
    return jnp.dot(a, b, preferred_element_type=F32)


def _dot_nt(a, b):
    return lax.dot_general(a, b, (((1,), (1,)), ((), ())), preferred_element_type=F32)


def _sigmoid(x):
    return 1.0 / (1.0 + jnp.exp(-x))


def _rms(x, g):
    return x * lax.rsqrt(jnp.mean(x * x, axis=-1, keepdims=True) + EPS) * g


def _const_spec(shape):
    nd = len(shape)
    return pl.BlockSpec(shape, lambda *_: (0,) * nd)


def _ada_kernel(c_ref, w_ref, b_ref, o_ref):
    c = c_ref[...]
    s = (c * _sigmoid(c)).astype(BF16)
    o_ref[...] = _dot(s, w_ref[...].astype(BF16)) + b_ref[...]


def _ada(c_all, w_ada, b_ada):
    n = c_all.shape[0]
    tn = 1536
    return pl.pallas_call(
        _ada_kernel,
        grid=(DEPTH, 6 * D_MODEL // tn),
        in_specs=[
            pl.BlockSpec((n, D_MODEL), lambda l, j: (0, 0)),
            pl.BlockSpec((None, D_MODEL, tn), lambda l, j: (l, 0, j)),
            pl.BlockSpec((None, 1, tn), lambda l, j: (l, 0, j)),
        ],
        out_specs=pl.BlockSpec((None, n, tn), lambda l, j: (l, 0, j)),
        out_shape=jax.ShapeDtypeStruct((DEPTH, n, 6 * D_MODEL), F32),
        compiler_params=_cparams(("parallel", "parallel")),
        name="ada",
    )(c_all, w_ada, b_ada.reshape(DEPTH, 1, 6 * D_MODEL))


def _proj_tail(h, w_ref, conv_ref, ga_ref, gb_ref, gate_ref):
    glu = _dot(h, w_ref[:, _OFF_GLU:_OFF_MG])
    conv_ref[...] = glu[:, :C_CONV] * _sigmoid(glu[:, C_CONV:])
    mg = _sigmoid(_dot(h, w_ref[:, _OFF_MG:_OFF_G]))
    ga_ref[...] = mg[:, :D_MODEL]
    gb_ref[...] = mg[:, D_MODEL:]
    gate_ref[...] = _sigmoid(_dot(h, w_ref[:, _OFF_G:_N_PROJ]))


def _proj_prompt_kernel(x_ref, sc_ref, sh_ref, g_ref, w_ref, wt_ref,
                        q_ref, kvt_ref, cmp_ref, conv_ref, ga_ref, gb_ref, gate_ref):
    h = (_rms(x_ref[...], g_ref[...]) * (1.0 + sc_ref[...]) + sh_ref[...]).astype(BF16)
    q_ref[...] = _dot(h, w_ref[:, _OFF_Q:_OFF_KV]).astype(BF16)
    kvt_ref[...] = _dot_nt(wt_ref[...], h)
    cmp_ref[...] = _dot(h, w_ref[:, _OFF_KV:_OFF_KV + 2 * KV_WIDTH])
    _proj_tail(h, w_ref, conv_ref, ga_ref, gb_ref, gate_ref)


def _proj_sample_kernel(x_ref, sc_ref, sh_ref, g_ref, w_ref,
                        q_ref, kv_ref, conv_ref, ga_ref, gb_ref, gate_ref):
    h = (_rms(x_ref[...], g_ref[...]) * (1.0 + sc_ref[...]) + sh_ref[...]).astype(BF16)
    q_ref[...] = _dot(h, w_ref[:, _OFF_Q:_OFF_KV]).astype(BF16)
    kv_ref[...] = _dot(h, w_ref[:, _OFF_KV:_OFF_GLU])
    _proj_tail(h, w_ref, conv_ref, ga_ref, gb_ref, gate_ref)


def _mod_spec(mod, tm, rows_per_batch):
    if mod.ndim == 3:
        return pl.BlockSpec((None, 1, D_MODEL), lambda i: (i * tm // rows_per_batch, 0, 0))
    return pl.BlockSpec((tm, D_MODEL), lambda i: (i, 0))


def _row_spec(tm, c):
    return pl.BlockSpec((tm, c), lambda i: (i, 0))


def _proj_tail_out(n, tm):
    specs = [_row_spec(tm, C_CONV), _row_spec(tm, D_MODEL), _row_spec(tm, D_MODEL), _row_spec(tm, LANE)]
    shapes = [jax.ShapeDtypeStruct((n, C_CONV), F32), jax.ShapeDtypeStruct((n, D_MODEL), F32),
              jax.ShapeDtypeStruct((n, D_MODEL), F32), jax.ShapeDtypeStruct((n, LANE), F32)]
    return specs, shapes


def _proj_prompt(x, sc, sh, gain, w, wt, tm, b, t):
    n = x.shape[0]
    per_b = t // tm
    tail_specs, tail_shapes = _proj_tail_out(n, tm)
    return pl.pallas_call(
        _proj_prompt_kernel,
        grid=(n // tm,),
        in_specs=[_row_spec(tm, D_MODEL), _mod_spec(sc, tm, t), _mod_spec(sh, tm, t),
                  _const_spec((1, D_MODEL)), _const_spec((D_MODEL, _N_PROJ)), _const_spec((_KV6, D_MODEL))],
        out_specs=[_row_spec(tm, D_MODEL),
                   pl.BlockSpec((None, _KV6, tm), lambda i: (i // per_b, 0, i % per_b)),
                   _row_spec(tm, 2 * KV_WIDTH)] + tail_specs,
        out_shape=[jax.ShapeDtypeStruct((n, D_MODEL), BF16),
                   jax.ShapeDtypeStruct((b, _KV6, t), F32),
                   jax.ShapeDtypeStruct((n, 2 * KV_WIDTH), F32)] + tail_shapes,
        compiler_params=_cparams(("parallel",)),
        name="proj",
    )(x, sc, sh, gain, w, wt)


def _proj_sample(x, sc, sh, gain, w, tm, rows_per_batch):
    n = x.shape[0]
    tail_specs, tail_shapes = _proj_tail_out(n, tm)
    return pl.pallas_call(
        _proj_sample_kernel,
        grid=(n // tm,),
        in_specs=[_row_spec(tm, D_MODEL), _mod_spec(sc, tm, rows_per_batch), _mod_spec(sh, tm, rows_per_batch),
                  _const_spec((1, D_MODEL)), _const_spec((D_MODEL, _N_PROJ))],
        out_specs=[_row_spec(tm, D_MODEL), _row_spec(tm, _KV6)] + tail_specs,
        out_shape=[jax.ShapeDtypeStruct((n, D_MODEL), BF16),
                   jax.ShapeDtypeStruct((n, _KV6), F32)] + tail_shapes,
        compiler_params=_cparams(("parallel",)),
        name="proj_sample",
    )(x, sc, sh, gain, w)


def _proj_weight(w_in_l):
    q = w_in_l[:, :1024] * ATTN_SCALE
    kv = w_in_l[:, 1024:2560]
    g = w_in_l[:, 2560:2608]
    glu = w_in_l[:, 2608:3632]
    mg = w_in_l[:, 3632:5680]
    pad = jnp.zeros((D_MODEL, _N_PROJ - _OFF_G - 48), F32)
    return jnp.concatenate([q, kv, glu, mg, g, pad], axis=1).astype(BF16), kv.T.astype(BF16)


_POS_PAIRS = L_CMP // 2


def _gelu_tanh(x):
    return x * (0.5 * (1.0 + jnp.tanh(GELU_C * (x + 0.044715 * (x * x * x)))))


def _compress_kernel(xe_ref, xo_ref, wc_ref, b1_ref, w2_ref, kc_ref, vc_ref, acc_ref):
    pp = pl.program_id(1)

    @pl.when(pp == 0)
    def _():
        acc_ref[...] = jnp.zeros_like(acc_ref)

    for kv in range(2):
        for gh in range(2):
            c0 = kv * KV_WIDTH + gh * LANE
            chunk = jnp.concatenate([xe_ref[:, c0:c0 + LANE], xo_ref[:, c0:c0 + LANE]], axis=1)
            a0 = (kv * 2 + gh) * 256
            acc_ref[:, a0:a0 + 256] += _dot(chunk.astype(BF16), wc_ref[kv])

    @pl.when(pp == _POS_PAIRS - 1)
    def _():
        for kv, o_ref in ((0, kc_ref), (1, vc_ref)):
            halves = []
            for gh in range(2):
                a0 = (kv * 2 + gh) * 256
                h = _gelu_tanh(acc_ref[:, a0:a0 + 256] + b1_ref[kv])
                halves.append(_dot(h.astype(BF16), w2_ref[kv]))
            o_ref[...] = jnp.concatenate(halves, axis=1)


def _compress_weights(w1, b1, w2):
    eye2 = jnp.eye(2, dtype=F32)
    w1r = w1.reshape(2, _POS_PAIRS, 2, HEAD_DIM, CMP_HIDDEN)
    wc = jnp.einsum("kpidh,gG->kpigdGh", w1r, eye2).reshape(2, _POS_PAIRS, 256, 256).astype(BF16)
    b1t = jnp.concatenate([b1, b1], axis=-1).reshape(2, 1, 256)
    w2e = jnp.einsum("khd,gG->kghGd", w2, eye2).reshape(2, 256, LANE).astype(BF16)
    return wc, b1t, w2e


def _compress(x2, wc, b1t, w2e, r):
    nb = x2.shape[0]
    blk = lambda off: pl.BlockSpec((r, 2 * KV_WIDTH), lambda i, pp: (i, 2 * pp + off))
    out = pl.BlockSpec((r, KV_WIDTH), lambda i, pp: (i, 0))
    return pl.pallas_call(
        _compress_kernel,
        grid=(nb // r, _POS_PAIRS),
        in_specs=[blk(0), blk(1),
                  pl.BlockSpec((2, None, 256, 256), lambda i, pp: (0, pp, 0, 0)),
                  _const_spec((2, 1, 256)), _const_spec((2, 256, LANE))],
        out_specs=[out, out],
        out_shape=[jax.ShapeDtypeStruct((nb, KV_WIDTH), F32)] * 2,
        scratch_shapes=[pltpu.VMEM((r, 1024), F32)],
        compiler_params=_cparams(("parallel", "arbitrary")),
        name="compress",
    )(x2, x2, wc, b1t, w2e)


_PAGE = 128
_PP = 32
_BPP = _PAGE // L_CMP
_PITCH = L_CMP + 4


def _compress_pool_kernel(x_ref, wc_ref, b1_ref, w2_ref, o_ref, xs_ref):
    def relayout(j, carry):
        r0 = pl.multiple_of(j * (_BPP * _PITCH), 8)
        for kv in range(2):
            x = x_ref[j, kv].T
            for gh in range(2):
                for b in range(_BPP):
                    xs_ref[kv * 2 + gh, pl.ds(r0 + b * _PITCH, L_CMP), :] = (
                        x[b * L_CMP:(b + 1) * L_CMP, gh * LANE:(gh + 1) * LANE])
        return carry

    lax.fori_loop(0, _PP, relayout, 0)
    nblk = _PP * _BPP
    for kv in range(2):
        for gh in range(2):
            acc = jnp.zeros((nblk, 256), F32)
            for pp in range(_POS_PAIRS):
                a0 = xs_ref[kv * 2 + gh, pl.ds(2 * pp, nblk, stride=_PITCH), :]
                a1 = xs_ref[kv * 2 + gh, pl.ds(2 * pp + 1, nblk, stride=_PITCH), :]
                chunk = jnp.concatenate([a0, a1], axis=1).astype(BF16)
                acc = acc + _dot(chunk, wc_ref[kv, pp])
            h = _gelu_tanh(acc + b1_ref[kv])
            c0 = kv * KV_WIDTH + gh * LANE
            o_ref[:, c0:c0 + LANE] = _dot(h.astype(BF16), w2_ref[kv])


def _compress_pool(cache_t, layer, wc, b1t, w2e):
    n_pool = cache_t.shape[1]
    return pl.pallas_call(
        _compress_pool_kernel,
        grid=(n_pool // _PP,),
        in_specs=[pl.BlockSpec((None, _PP, 2, KV_WIDTH, _PAGE), lambda i: (layer, i, 0, 0, 0)),
                  _const_spec((2, _POS_PAIRS, 256, 256)), _const_spec((2, 1, 256)),
                  _const_spec((2, 256, LANE))],
        out_specs=pl.BlockSpec((_PP * _BPP, 2 * KV_WIDTH), lambda i: (i, 0)),
        out_shape=jax.ShapeDtypeStruct((n_pool * _BPP, 2 * KV_WIDTH), F32),
        scratch_shapes=[pltpu.VMEM((4, _PP * _BPP * _PITCH, LANE), F32)],
        compiler_params=_cparams(("parallel",)),
        name="compress_pool",
    )(cache_t, wc, b1t, w2e)


def _topk_member(score, n_rows):
    sb = lax.broadcasted_iota(jnp.int32, score.shape, 0)
    cnt = jnp.zeros(score.shape, jnp.int32)
    for jj in range(n_rows):
        row = score[jj:jj + 1, :]
        beats = (row > score) | ((row == score) & (jj < sb))
        cnt = cnt + jnp.where(beats, 1, 0)
    return cnt < N_SEL


_TQ = 256


def _select_kernel(q_ref, kcp_ref, qa_ref):
    t0 = pl.program_id(2) * _TQ
    qpos = t0 + lax.broadcasted_iota(jnp.int32, (1, _TQ), 1)
    r = lax.broadcasted_iota(jnp.int32, (2 * 32, 1), 0)
    blk = jnp.where(r < 32, 2 * r, 2 * (r - 32) + 1)
    mask_c = ((blk + 1) * L_CMP - 1) <= qpos
    q = q_ref[...]
    kcp = kcp_ref[...]
    p_slc = jnp.zeros((32, _TQ), F32)
    for j in range(GROUP):
        s = _dot_nt(kcp, q[:, j * HEAD_DIM:(j + 1) * HEAD_DIM])
        s = jnp.where(mask_c, s, -jnp.inf)
        m = jnp.max(s, axis=0, keepdims=True)
        m = jnp.where(m == -jnp.inf, 0.0, m)
        p = jnp.exp(s - m)
        d = jnp.sum(p, axis=0, keepdims=True)
        p = p / jnp.where(d > 0, d, 1.0)
        p_slc = p_slc + (p[:32] + p[32:])
    sb = lax.broadcasted_iota(jnp.int32, (32, 1), 0)
    cur = qpos // L_SLC
    forced = (sb == 0) | (sb == cur) | (sb == cur - 1)
    score = p_slc + jnp.where(forced, FORCE_BONUS, 0.0)
    score = jnp.where(sb * L_SLC <= qpos, score, NEG_SCORE)
    sel = _topk_member(score, 32)
    bias_t = jnp.where(sel, 0.0, MASK_BIAS)
    bias = jnp.concatenate([bias_t, jnp.zeros((LANE - 32, _TQ), F32)], axis=0).T
    for j in range(GROUP):
        qh = q[:, j * HEAD_DIM:(j + 1) * HEAD_DIM].astype(F32)
        qa_ref[j] = jnp.concatenate([qh, bias[:, :LANE - HEAD_DIM]], axis=1).astype(BF16)


def _select(q, kcp, b, t):
    return pl.pallas_call(
        _select_kernel,
        grid=(b, N_KV, t // _TQ),
        in_specs=[pl.BlockSpec((None, _TQ, KV_WIDTH), lambda bi, g, i: (bi, i, g)),
                  pl.BlockSpec((None, None, 64, HEAD_DIM), lambda bi, g, i: (bi, g, 0, 0))],
        out_specs=pl.BlockSpec((None, None, GROUP, _TQ, LANE), lambda bi, g, i: (bi, g, 0, i, 0)),
        out_shape=jax.ShapeDtypeStruct((b, N_KV, GROUP, t, LANE), BF16),
        compiler_params=_cparams(("parallel", "parallel", "parallel")),
        name="select",
    )(q, kcp)


_TK = 256
_ROWS = GROUP * _TQ
_ONES_ROW = HEAD_DIM


def _two_pass_attention(q, qi, row_t, k_scr, v_scr, n_tiles, far_window_mask, s_all, m_ref, acc_ref):
    kiota = lax.broadcasted_iota(jnp.int32, (1, _TK), 1)

    def scores(i, mask):
        kt = qi - i
        s = _dot(q, k_scr[kt])
        if mask is not None:
            s = jnp.where(mask(row_t - (kt * _TK + kiota)), s, -jnp.inf)
        s_all[i] = s
        return jnp.maximum(s[:, :LANE], s[:, LANE:])

    m_ref[...] = scores(0, lambda dist: dist >= 0)

    def pass1(i, carry):
        mask = (lambda dist: dist < WINDOW) if far_window_mask else None
        m_ref[...] = jnp.maximum(m_ref[...], scores(i, mask))
        return carry

    lax.fori_loop(1, n_tiles, pass1, 0)
    m_ref[...] = jnp.broadcast_to(jnp.max(m_ref[...], axis=-1, keepdims=True), m_ref.shape)
    acc_ref[...] = jnp.zeros(acc_ref.shape, F32)

    def pass2(i, carry):
        s = s_all[i]
        m = m_ref[...]
        p = jnp.exp(jnp.concatenate([s[:, :LANE] - m, s[:, LANE:] - m], axis=1)).astype(BF16)
        acc_ref[...] += _dot_nt(p, v_scr[qi - i])
        return carry

    lax.fori_loop(0, n_tiles, pass2, 0)
    acc = acc_ref[...]
    return acc[:, :HEAD_DIM] / acc[:, _ONES_ROW:_ONES_ROW + 1]


def _attn_kernel(qa_ref, ks_ref, vs_ref, kw_ref, vw_ref, kc_ref, vc_ref, g_ref, o_ref,
                 ks_scr, vs_scr, kw_scr, vw_scr, s_all, m_ref, acc_ref):
    qi = pl.program_id(2)
    n_kt = ks_scr.shape[0]

    @pl.when(qi == 0)
    def _():
        blk_row = lax.broadcasted_iota(jnp.int32, (32, _TK), 0)
        ones_row = jnp.where(lax.broadcasted_iota(jnp.int32, (16, _TK), 0) == 0, 1.0, 0.0).astype(BF16)
        for kt in range(n_kt):
            cols = slice(kt * _TK, (kt + 1) * _TK)
            lane_blk = (kt * _TK + lax.broadcasted_iota(jnp.int32, (32, _TK), 1)) // L_SLC
            ks_scr[kt, 0:HEAD_DIM, :] = ks_ref[:, cols].astype(BF16)
            ks_scr[kt, HEAD_DIM:HEAD_DIM + 32, :] = jnp.where(lane_blk == blk_row, 1.0, 0.0).astype(BF16)
            ks_scr[kt, HEAD_DIM + 32:, :] = jnp.zeros((LANE - HEAD_DIM - 32, _TK), BF16)
            kw_scr[kt, 0:HEAD_DIM, :] = kw_ref[:, cols].astype(BF16)
            kw_scr[kt, HEAD_DIM:, :] = jnp.zeros((LANE - HEAD_DIM, _TK), BF16)
            for v_ref, v_scr in ((vs_ref, vs_scr), (vw_ref, vw_scr)):
                v_scr[kt, 0:HEAD_DIM, :] = v_ref[:, cols].astype(BF16)
                v_scr[kt, HEAD_DIM:HEAD_DIM + 16, :] = ones_row
                v_scr[kt, HEAD_DIM + 16:, :] = jnp.zeros((LANE - HEAD_DIM - 16, _TK), BF16)

    t0 = qi * _TQ
    q = qa_ref[...].reshape(_ROWS, LANE)
    row_t = t0 + (lax.broadcasted_iota(jnp.int32, (_ROWS, 1), 0) & (_TQ - 1))

    s = _dot(q, kc_ref[...])
    cb_end = (lax.broadcasted_iota(jnp.int32, (1, 64), 1) + 1) * L_CMP - 1
    s = jnp.where(cb_end <= row_t, s, -jnp.inf)
    m = jnp.max(s, axis=-1, keepdims=True)
    m = jnp.where(m == -jnp.inf, 0.0, m)
    oc = _dot_nt(jnp.exp(s - m).astype(BF16), vc_ref[...])
    den = oc[:, _ONES_ROW:_ONES_ROW + 1]
    o_cmp = oc[:, :HEAD_DIM] / jnp.where(den > 0, den, 1.0)

    o_slc = _two_pass_attention(q, qi, row_t, ks_scr, vs_scr, qi + 1, False, s_all, m_ref, acc_ref)
    o_win = _two_pass_attention(q, qi, row_t, kw_scr, vw_scr, jnp.minimum(qi, WINDOW // _TK) + 1,
                                True, s_all, m_ref, acc_ref)

    g = g_ref[...]
    heads = []
    for j in range(GROUP):
        rs = slice(j * _TQ, (j + 1) * _TQ)
        c = N_BRANCH * j
        heads.append(g[:, c:c + 1] * o_cmp[rs] + g[:, c + 1:c + 2] * o_slc[rs]
                     + g[:, c + 2:c + 3] * o_win[rs])
    o_ref[...] = jnp.concatenate(heads, axis=1)


def _attn(qa, kvt, kc, vc, gates, b, t):
    head_t = lambda c: pl.BlockSpec((None, None, HEAD_DIM, t), lambda bi, g, i: (bi, N_KV * c + g, 0, 0))
    summ = pl.BlockSpec((None, None, LANE, 64), lambda bi, g, i: (bi, g, 0, 0))
    n_kt = t // _TK
    tiles = pltpu.VMEM((n_kt, LANE, _TK), BF16)
    return pl.pallas_call(
        _attn_kernel,
        grid=(b, N_KV, t // _TQ),
        in_specs=[pl.BlockSpec((None, None, GROUP, _TQ, LANE), lambda bi, g, i: (bi, g, 0, i, 0)),
                  head_t(2), head_t(3), head_t(4), head_t(5), summ, summ,
                  pl.BlockSpec((None, None, _TQ, LANE), lambda bi, g, i: (bi, g, i, 0))],
        out_specs=pl.BlockSpec((None, _TQ, KV_WIDTH), lambda bi, g, i: (bi, i, g)),
        out_shape=jax.ShapeDtypeStruct((b, t, D_MODEL), F32),
        scratch_shapes=[tiles, tiles, tiles, tiles,
                        pltpu.VMEM((n_kt, _ROWS, _TK), F32),
                        pltpu.VMEM((_ROWS, LANE), F32), pltpu.VMEM((_ROWS, LANE), F32)],
        compiler_params=_cparams(("arbitrary", "arbitrary", "arbitrary")),
        name="attn",
    )(qa, kvt, kvt, kvt, kvt, kc, vc, gates)


_N_PAGES = 16
_PAST = _N_PAGES * _PAGE
_SROWS = GROUP * N_KV * 4
_NEWPAD = 128
_NSB_PAD = 40
_NEWROWS = 8


def _diag_heads(o, row_g):
    col_g = lax.broadcasted_iota(jnp.int32, (1, KV_WIDTH), 1) // HEAD_DIM
    om = jnp.where(col_g == row_g, o, 0.0)
    return (om[:, 0:64] + om[:, 64:128]) + (om[:, 128:192] + om[:, 192:256])


def _softmax_rows(s):
    m = jnp.max(s, axis=-1, keepdims=True)
    m = jnp.where(m == -jnp.inf, 0.0, m)
    p = jnp.exp(s - m)
    d = jnp.sum(p, axis=-1, keepdims=True)
    return p / jnp.where(d > 0, d, 1.0)


def _softmax_two(s_old, s_new):
    m = jnp.maximum(jnp.max(s_old, axis=-1, keepdims=True), jnp.max(s_new, axis=-1, keepdims=True))
    m = jnp.where(m == -jnp.inf, 0.0, m)
    p_old = jnp.exp(s_old - m)
    p_new = jnp.exp(s_new - m)
    d = jnp.sum(p_old, axis=-1, keepdims=True) + jnp.sum(p_new, axis=-1, keepdims=True)
    inv = 1.0 / jnp.where(d > 0, d, 1.0)
    return (p_old * inv).astype(BF16), (p_new * inv).astype(BF16)


def _sample_attn_kernel(dt, pt_ref, qbd_ref, *refs):
    del pt_ref
    page_refs = refs[:_N_PAGES]
    sum_refs = refs[_N_PAGES:2 * _N_PAGES]
    kvn_ref, wold_ref, wnewt_ref, g_ref, oh_ref, o_ref, wout_ref, sum_scr, kcp_scr = refs[2 * _N_PAGES:]

    qbd = qbd_ref[...]
    rows = lax.broadcasted_iota(jnp.int32, (_SROWS, 1), 0)
    row_t = rows % dt
    row_g = (rows // dt) % N_KV

    per_page = _PAGE // L_CMP
    for j in range(_N_PAGES):
        sum_scr[per_page * j:per_page * (j + 1), :] = sum_refs[j][...]
        for i in range(per_page):
            dst = (i % 2) * 32 + (per_page * j + i) // 2
            kcp_scr[dst:dst + 1, :] = sum_refs[j][i:i + 1, :KV_WIDTH]
    kvc = sum_scr[...]
    p = _softmax_rows(_dot_nt(qbd, kvc[:, :KV_WIDTH].astype(BF16)))
    o_cmp = _diag_heads(_dot(p.astype(BF16), kvc[:, KV_WIDTH:].astype(BF16)), row_g)

    s_t = _dot_nt(kcp_scr[...].astype(BF16), qbd)
    m = jnp.max(s_t, axis=0, keepdims=True)
    pt = jnp.exp(s_t - m)
    pt = pt / jnp.sum(pt, axis=0, keepdims=True)
    pp = pt[:32] + pt[32:]
    nq = N_KV * dt
    p_slc = (pp[:, 0:nq] + pp[:, nq:2 * nq]) + (pp[:, 2 * nq:3 * nq] + pp[:, 3 * nq:4 * nq])
    p_slc = jnp.concatenate([p_slc, jnp.zeros((_NSB_PAD - 32, nq), F32)], axis=0)
    sb = lax.broadcasted_iota(jnp.int32, (_NSB_PAD, 1), 0)
    qpos = _PAST + lax.broadcasted_iota(jnp.int32, (1, nq), 1) % dt
    cur = qpos // L_SLC
    forced = (sb == 0) | (sb == cur) | (sb == cur - 1)
    score = p_slc + jnp.where(forced, FORCE_BONUS, 0.0)
    score = jnp.where(sb * L_SLC <= qpos, score, NEG_SCORE)
    score = jnp.where(sb < _PAST // L_SLC + 1, score, -jnp.inf)
    sel_t = _topk_member(score, _PAST // L_SLC + 1)
    bias_t = jnp.where(sel_t, 0.0, MASK_BIAS)
    bias_t = jnp.concatenate([bias_t, jnp.zeros((LANE - _NSB_PAD, nq), F32)], axis=0)
    bias_t = jnp.concatenate([bias_t, jnp.zeros((LANE, LANE - nq), F32)], axis=1)
    bias = bias_t.T[:nq]
    bias = jnp.concatenate([bias] * GROUP, axis=0).astype(BF16)

    knew = kvn_ref[...]
    zpad = jnp.zeros((_NEWPAD - _NEWROWS, KV_WIDTH), F32)
    new_tm = lambda c: jnp.concatenate([knew[:, c * KV_WIDTH:(c + 1) * KV_WIDTH], zpad], axis=0).astype(BF16)
    nidx = lax.broadcasted_iota(jnp.int32, (1, _NEWPAD), 1)
    kt_all = jnp.concatenate([r[0] for r in page_refs], axis=1).astype(BF16)
    vt_all = jnp.concatenate([r[1] for r in page_refs], axis=1).astype(BF16)
    oh = oh_ref[...]
    s_old = _dot(qbd, kt_all) + _dot(bias, oh[:, :_PAST])
    s_new = _dot_nt(qbd, new_tm(2)) + _dot(bias, oh[:, _PAST:])
    s_new = jnp.where(nidx <= row_t, s_new, -jnp.inf)
    p_old, p_new = _softmax_two(s_old, s_new)
    o_slc = _diag_heads(_dot_nt(p_old, vt_all) + _dot(p_new, new_tm(3)), row_g)

    wb = wold_ref.shape[2]
    s_old = _dot(qbd, wold_ref[0].astype(BF16))
    widx = lax.broadcasted_iota(jnp.int32, (1, wb), 1)
    dist = row_t + (wb - widx)
    s_old = jnp.where((dist >= 0) & (dist < WINDOW), s_old, -jnp.inf)
    s_new = _dot_nt(qbd, new_tm(4))
    dist = row_t - nidx
    s_new = jnp.where((dist >= 0) & (dist < WINDOW), s_new, -jnp.inf)
    p_old, p_new = _softmax_two(s_old, s_new)
    o_win = _diag_heads(_dot_nt(p_old, wold_ref[1].astype(BF16)) + _dot(p_new, new_tm(5)), row_g)

    g = g_ref[...]
    o_ref[...] = g[:, 0:1] * o_cmp + g[:, 1:2] * o_slc + g[:, 2:3] * o_win

    lane = lax.broadcasted_iota(jnp.int32, (1, LANE), 1)
    for c in range(2):
        shifted = pltpu.roll(wold_ref[c], wb - dt, axis=1)
        wout_ref[c, :, 0:wb - LANE] = shifted[:, 0:wb - LANE]
        wout_ref[c, :, wb - LANE:wb] = jnp.where(lane >= LANE - dt, wnewt_ref[c], shifted[:, wb - LANE:wb])


def _sample_attn(page_table, qbd, cache_t, layer, kvc_pool, kvn, win_t, wnewt, grow, onehot, dt):
    nb = qbd.shape[0]
    wb = win_t.shape[-1]
    kvn = jnp.pad(kvn, ((0, 0), (0, _NEWROWS - dt), (0, 0)))
    page = lambda j: pl.BlockSpec((None, None, 2, KV_WIDTH, _PAGE),
                                  lambda b, pt: (layer, pt[b * _N_PAGES + j], 1, 0, 0))
    summ = lambda j: pl.BlockSpec((None, _BPP, 2 * KV_WIDTH),
                                  lambda b, pt: (pt[b * _N_PAGES + j], 0, 0))
    per_seq = lambda *shape: pl.BlockSpec((None,) + shape, lambda b, pt: (b,) + (0,) * len(shape))
    grid_spec = pltpu.PrefetchScalarGridSpec(
        num_scalar_prefetch=1,
        grid=(nb,),
        in_specs=[per_seq(_SROWS, KV_WIDTH)]
        + [page(j) for j in range(_N_PAGES)] + [summ(j) for j in range(_N_PAGES)]
        + [per_seq(_NEWROWS, _KV6),
           pl.BlockSpec((None, None, 2, KV_WIDTH, wb), lambda b, pt: (layer, b, 0, 0, 0)),
           per_seq(2, KV_WIDTH, LANE), per_seq(_SROWS, LANE),
           pl.BlockSpec((LANE, _PAST + _NEWPAD), lambda b, pt: (0, 0))],
        out_specs=[per_seq(_SROWS, HEAD_DIM), per_seq(2, KV_WIDTH, wb)],
        scratch_shapes=[pltpu.VMEM((_BPP * _N_PAGES, 2 * KV_WIDTH), F32),
                        pltpu.VMEM((_BPP * _N_PAGES, KV_WIDTH), F32)],
    )
    return pl.pallas_call(
        functools.partial(_sample_attn_kernel, dt),
        grid_spec=grid_spec,
        out_shape=[jax.ShapeDtypeStruct((nb, _SROWS, HEAD_DIM), F32),
                   jax.ShapeDtypeStruct((nb, 2, KV_WIDTH, wb), F32)],
        compiler_params=_cparams(("parallel",)),
        name="sample_attn",
    )(page_table.reshape(-1), qbd, *([cache_t] * _N_PAGES), *([kvc_pool] * _N_PAGES),
      kvn, win_t, wnewt, grow, onehot)


_TT = 256
_HALO = 32
_RC = 64
_SUB = 8


def _dwconv_kernel(cur_ref, prev_ref, w_ref, b_ref, y_ref, h_ref):
    prev = prev_ref[...]
    h_ref[0:_HALO, :] = jnp.where(pl.program_id(1) == 0, 0.0, prev)
    h_ref[_HALO:_HALO + _TT, :] = cur_ref[...]
    h_ref[_HALO + _TT:, :] = jnp.zeros((_SUB, C_CONV), F32)
    off = _HALO - (CONV_W - 1)
    for r0 in range(0, _TT, _RC):
        acc = jnp.broadcast_to(b_ref[...], (_RC, C_CONV))
        for s in range(_SUB):
            group = None
            for a in range((_HALO + _SUB) // _SUB):
                k = _SUB * a + s - off
                if 0 <= k < CONV_W:
                    rows = slice(r0 + _SUB * a, r0 + _SUB * a + _RC + _SUB)
                    term = w_ref[k:k + 1, :] * h_ref[rows, :]
                    group = term if group is None else group + term
            acc = acc + group[s:s + _RC, :]
        y_ref[r0:r0 + _RC, :] = acc


def _dwconv(conv_in, w_pad, b_dw):
    b, t, _ = conv_in.shape
    return pl.pallas_call(
        _dwconv_kernel,
        grid=(b, t // _TT),
        in_specs=[pl.BlockSpec((None, _TT, C_CONV), lambda bi, i: (bi, i, 0)),
                  pl.BlockSpec((None, _HALO, C_CONV),
                               lambda bi, i: (bi, jnp.maximum(i * (_TT // _HALO) - 1, 0), 0)),
                  _const_spec((_HALO, C_CONV)), _const_spec((1, C_CONV))],
        out_specs=pl.BlockSpec((None, _TT, C_CONV), lambda bi, i: (bi, i, 0)),
        out_shape=jax.ShapeDtypeStruct((b, t, C_CONV), F32),
        scratch_shapes=[pltpu.VMEM((_HALO + _TT + _SUB, C_CONV), F32)],
        compiler_params=_cparams(("parallel", "parallel")),
        name="dwconv",
    )(conv_in, conv_in, w_pad, b_dw)


_SB = 8


def _dwconv_sample_kernel(h_ref, w_ref, b_ref, y_ref):
    dt = y_ref.shape[0]
    w = w_ref[0:CONV_W, :]
    for t in range(dt):
        y_ref[t:t + 1, :] = jnp.sum(h_ref[t:t + CONV_W, :] * w, axis=0, keepdims=True) + b_ref[...]


def _dwconv_sample(h2, w_pad, b_dw, b, c):
    rows = h2.shape[0]
    dt = rows - (CONV_W - 1)
    w_t = jnp.tile(w_pad, (1, _SB))
    b_t = jnp.tile(b_dw, (1, _SB))
    return pl.pallas_call(
        _dwconv_sample_kernel,
        grid=(b // _SB,),
        in_specs=[pl.BlockSpec((rows, _SB * c), lambda i: (0, i)),
                  _const_spec((_HALO, _SB * c)), _const_spec((1, _SB * c))],
        out_specs=pl.BlockSpec((dt, _SB * c), lambda i: (0, i)),
        out_shape=jax.ShapeDtypeStruct((dt, b * c), F32),
        compiler_params=_cparams(("parallel",)),
        name="dwconv_sample",
    )(h2, w_t, b_t)


def _mix_out_kernel(y_ref, lng_ref, lnb_ref, wpw_ref, on_ref, ga_ref, gb_ref, wo_ref,
                    x_ref, gt_ref, gain_ref, o_ref):
    y = y_ref[...]
    yc = y - jnp.mean(y, axis=-1, keepdims=True)
    var = jnp.mean(yc * yc, axis=-1, keepdims=True)
    yn = yc * lax.rsqrt(var + EPS) * lng_ref[...] + lnb_ref[...]
    o_conv = _dot((yn * _sigmoid(yn)).astype(BF16), wpw_ref[...])
    merged = ga_ref[...] * on_ref[...] + gb_ref[...] * o_conv
    yo = _dot(merged.astype(BF16), wo_ref[...])
    o_ref[...] = x_ref[...] + gt_ref[...] * _rms(yo, gain_ref[...])


def _mix_out(ydw, lng, lnb, wpw, o_nsa, ga, gb, wo, x, gt, gain, tm, rows_per_batch):
    n = x.shape[0]
    row = lambda c: _row_spec(tm, c)
    return pl.pallas_call(
        _mix_out_kernel,
        grid=(n // tm,),
        in_specs=[row(C_CONV), _const_spec((1, C_CONV)), _const_spec((1, C_CONV)),
                  _const_spec((C_CONV, D_MODEL)), row(D_MODEL), row(D_MODEL), row(D_MODEL),
                  _const_spec((D_MODEL, D_MODEL)), row(D_MODEL), _mod_spec(gt, tm, rows_per_batch),
                  _const_spec((1, D_MODEL))],
        out_specs=row(D_MODEL),
        out_shape=jax.ShapeDtypeStruct((n, D_MODEL), F32),
        compiler_params=_cparams(("parallel",)),
        name="mix_out",
    )(ydw, lng, lnb, wpw, o_nsa, ga, gb, wo, x, gt, gain)


_FC = D_FFN // 2


def _ffn_kernel(x_ref, sc_ref, sh_ref, gt_ref, g2_ref, g3_ref, wi_ref, wo_ref, o_ref):
    x = x_ref[...]
    h = (_rms(x, g2_ref[...]) * (1.0 + sc_ref[...]) + sh_ref[...]).astype(BF16)
    y = jnp.zeros(x.shape, F32)
    for c in range(0, D_FFN, _FC):
        a = _dot(h, wi_ref[:, c:c + _FC])
        b = _dot(h, wi_ref[:, D_FFN + c:D_FFN + c + _FC])
        u = (a * _sigmoid(a) * b).astype(BF16)
        y = y + _dot(u, wo_ref[c:c + _FC, :])
    o_ref[...] = x + gt_ref[...] * _rms(y, g3_ref[...])


def _ffn(x, sc, sh, gt, g2, g3, wi, wo, tm, rows_per_batch):
    n = x.shape[0]
    row = _row_spec(tm, D_MODEL)
    mod = lambda a: _mod_spec(a, tm, rows_per_batch)
    return pl.pallas_call(
        _ffn_kernel,
        grid=(n // tm,),
        in_specs=[row, mod(sc), mod(sh), mod(gt), _const_spec((1, D_MODEL)), _const_spec((1, D_MODEL)),
                  _const_spec((D_MODEL, 2 * D_FFN)), _const_spec((D_FFN, D_MODEL))],
        out_specs=row,
        out_shape=jax.ShapeDtypeStruct((n, D_MODEL), F32),
        compiler_params=_cparams(("parallel",)),
        name="ffn",
    )(x, sc, sh, gt, g2, g3, wi, wo)


def _summaries_t(s, b, n_cb, ones_row):
    st = s.reshape(b, n_cb, N_KV, HEAD_DIM).transpose(0, 2, 3, 1)
    extra = jnp.zeros((b, N_KV, LANE - HEAD_DIM, n_cb), F32)
    if ones_row:
        extra = extra.at[:, :, 0, :].set(1.0)
    return jnp.concatenate([st, extra], axis=2).astype(BF16)


def _prompt_mixer(q, kvt, gates, kc, vc, b, t):
    n_cb = t // L_CMP
    kch = kc.reshape(b, n_cb, N_KV, HEAD_DIM).transpose(0, 2, 1, 3).astype(BF16)
    kcp = jnp.concatenate([kch[:, :, 0::2], kch[:, :, 1::2]], axis=2)
    g4 = gates[:, :N_HEADS * N_BRANCH].reshape(b, t, N_KV, GROUP * N_BRANCH).transpose(0, 2, 1, 3)
    g4 = jnp.concatenate([g4, jnp.zeros((b, N_KV, t, LANE - GROUP * N_BRANCH), F32)], axis=-1)
    qa = _select(q.reshape(b, t, D_MODEL), kcp, b, t)
    o = _attn(qa, kvt.reshape(b, 6 * N_KV, HEAD_DIM, t), _summaries_t(kc, b, n_cb, False),
              _summaries_t(vc, b, n_cb, True), g4, b, t)
    return o.reshape(b * t, D_MODEL)


def _sample_mixer(q, kv, gates, kvc_pool, cache_t, layer, page_table, win_t, b, dt):
    qr = q.reshape(b, dt, N_KV, GROUP, HEAD_DIM).transpose(0, 3, 2, 1, 4)
    eye = jnp.eye(N_KV, dtype=BF16)
    qbd = (qr[:, :, :, :, None, :] * eye[None, None, :, None, :, None]).reshape(b, _SROWS, KV_WIDTH)
    gr = gates[:, :N_HEADS * N_BRANCH].reshape(b, dt, N_KV, GROUP, N_BRANCH).transpose(0, 3, 2, 1, 4)
    gr = gr.reshape(b, _SROWS, N_BRANCH)
    gr = jnp.concatenate([gr, jnp.zeros((b, _SROWS, LANE - N_BRANCH), F32)], axis=-1)
    nkeys = _PAST + _NEWPAD
    onehot = (jnp.arange(nkeys)[None, :] // L_SLC == jnp.arange(LANE)[:, None]).astype(BF16)
    kv3 = kv.reshape(b, dt, _KV6)
    wnewt = kv3[:, :, 4 * KV_WIDTH:].transpose(0, 2, 1).reshape(b, 2, KV_WIDTH, dt)
    wnewt = jnp.pad(wnewt, ((0, 0), (0, 0), (0, 0), (LANE - dt, 0)))
    o, wout = _sample_attn(page_table, qbd, cache_t, layer, kvc_pool, kv3, win_t, wnewt, gr, onehot, dt)
    o = o.reshape(b, GROUP, N_KV, dt, HEAD_DIM).transpose(0, 3, 2, 1, 4)
    return o.reshape(b * dt, D_MODEL), wout


def _feature_major(a):
    lead = a.shape[:-4]
    rows, p = a.shape[-4], a.shape[-3]
    nd = len(lead)
    perm = tuple(range(nd)) + (nd + 1, nd + 2, nd + 3, nd)
    return a.transpose(perm).reshape(lead + (p, KV_WIDTH, rows))


def _token_major(a_t, p):
    b, _, rows = a_t.shape
    return a_t.reshape(b, p, N_KV, HEAD_DIM, rows).transpose(0, 4, 1, 2, 3)


def kernel(x_prompt, x_sample, cache_kv, state_win, state_conv, page_table, c_prompt, c_sample,
           w_ada, b_ada, norm_gain, w_in, w_cmp1, b_cmp1, w_cmp2, w_dw, b_dw, ln_conv_g,
           ln_conv_b, w_pw2, w_out, w_ffn_in, w_ffn_out):
    bp, t, _ = x_prompt.shape
    bs, dt, _ = x_sample.shape
    n_pool = cache_kv.shape[1]
    tm_p, tm_s = 256, 256

    ada = _ada(jnp.concatenate([c_prompt, c_sample], axis=0), w_ada, b_ada)
    cache_t = _feature_major(cache_kv)
    win_t = _feature_major(state_win)
    conv_t = state_conv.transpose(0, 2, 1, 3)

    xp = x_prompt.reshape(bp * t, D_MODEL)
    xs = x_sample.reshape(bs * dt, D_MODEL)
    kv_p, kv_s, win_p, win_s, conv_p, conv_s = [], [], [], [], [], []
    for l in range(DEPTH):
        mods_p = [m.reshape(bp, 1, D_MODEL) for m in jnp.split(ada[l, :bp], 6, axis=-1)]
        mods_s = [jnp.repeat(m, dt, axis=0) for m in jnp.split(ada[l, bp:], 6, axis=-1)]
        gain = norm_gain[l].reshape(4, 1, D_MODEL)
        wproj, wkv_t = _proj_weight(w_in[l])
        wc, b1t, w2e = _compress_weights(w_cmp1[l], b_cmp1[l], w_cmp2[l])
        w_pad = jnp.concatenate([w_dw[l], jnp.zeros((_HALO - CONV_W, C_CONV), F32)], axis=0)
        bdw = b_dw[l].reshape(1, C_CONV)
        lng = ln_conv_g[l].reshape(1, C_CONV)
        lnb = ln_conv_b[l].reshape(1, C_CONV)
        wpw = w_pw2[l].astype(BF16)
        wo = w_out[l].astype(BF16)
        wfi = w_ffn_in[l].astype(BF16)
        wfo = w_ffn_out[l].astype(BF16)

        sh_m, sc_m, gt_m, sh_f, sc_f, gt_f = mods_p
        q, kvt, cmp_tm, conv_in, ga, gb, gates = _proj_prompt(xp, sc_m, sh_m, gain[0], wproj, wkv_t,
                                                              tm_p, bp, t)
        n_blk = bp * t // L_CMP
        kc, vc = _compress(cmp_tm.reshape(n_blk, L_CMP * 2 * KV_WIDTH), wc, b1t, w2e, n_blk)
        o_nsa = _prompt_mixer(q, kvt, gates, kc, vc, bp, t)
        conv_in3 = conv_in.reshape(bp, t, C_CONV)
        ydw = _dwconv(conv_in3, w_pad, bdw).reshape(bp * t, C_CONV)
        xp = _mix_out(ydw, lng, lnb, wpw, o_nsa, ga, gb, wo, xp, gt_m, gain[1], tm_p, t)
        xp = _ffn(xp, sc_f, sh_f, gt_f, gain[2], gain[3], wfi, wfo, tm_p, t)
        kv_p.append(_token_major(kvt[:, :4 * KV_WIDTH], 4))
        win_p.append(_token_major(kvt[:, 4 * KV_WIDTH:, t - min(WINDOW, t):], 2))
        conv_p.append(conv_in3[:, t - (CONV_W - 1):])

        sh_m, sc_m, gt_m, sh_f, sc_f, gt_f = mods_s
        q, kv, conv_in, ga, gb, gates = _proj_sample(xs, sc_m, sh_m, gain[0], wproj, tm_s, dt)
        kvc_pool = _compress_pool(cache_t, l, wc, b1t, w2e).reshape(n_pool, _BPP, 2 * KV_WIDTH)
        o_nsa, wout = _sample_mixer(q, kv, gates, kvc_pool, cache_t, l, page_table, win_t, bs, dt)
        h2 = jnp.concatenate([conv_t[l], conv_in.reshape(bs, dt, C_CONV).transpose(1, 0, 2)], axis=0)
        ydw = _dwconv_sample(h2.reshape(CONV_W - 1 + dt, bs * C_CONV), w_pad, bdw, bs, C_CONV)
        ydw = ydw.reshape(dt, bs, C_CONV).transpose(1, 0, 2).reshape(bs * dt, C_CONV)
        xs = _mix_out(ydw, lng, lnb, wpw, o_nsa, ga, gb, wo, xs, gt_m, gain[1], tm_s, dt)
        xs = _ffn(xs, sc_f, sh_f, gt_f, gain[2], gain[3], wfi, wfo, tm_s, dt)
        kv_s.append(kv[:, :4 * KV_WIDTH].reshape(bs, dt, 4, N_KV, HEAD_DIM))
        win_s.append(_token_major(wout.reshape(bs, 2 * KV_WIDTH, -1), 2))
        conv_s.append(h2[dt:].transpose(1, 0, 2))

    return (xp.reshape(bp, t, D_MODEL), xs.reshape(bs, dt, D_MODEL),
            jnp.stack(kv_p), jnp.stack(kv_s), jnp.stack(win_p), jnp.stack(win_s),
            jnp.stack(conv_p), jnp.stack(conv_s))
```
